```python
import math
import jax, jax.numpy as jnp
from jax import lax
import numpy as np

D_MODEL = 4096
BATCH = 1
SEQ = 8192
DEPTH = 1
DEC_BATCH = 2
DEC_SEQ = 8192
PAST_LEN = 128

H_DIFF = 8
D_DIFF = 128
DIFF_WIDTH = H_DIFF * 2 * D_DIFF
H_RET = 8
DK_RET = 256
DV_RET = 256
RET_WIDTH = H_RET * DV_RET
MIX_WIDTH = DIFF_WIDTH + RET_WIDTH
IN_SPLITS = (
    H_DIFF * 2 * D_DIFF,
    H_DIFF * 2 * D_DIFF,
    H_DIFF * 2 * D_DIFF,
    H_RET * DK_RET,
    H_RET * DK_RET,
    H_RET * DV_RET,
    RET_WIDTH,
)
IN_IS_VALUE = (False, False, True, False, False, True, False)
IN_WIDTH = sum(IN_SPLITS)
N_EXPERTS = 16
EC_CAPACITY = 2
D_FF_EXPERT = 8192
ROPE_THETA = 10000.0
Q_BLOCK = 128
RET_CHUNK = 128
NORM_EPS = 1e-5
DEEPNORM_ALPHA = (2.0 * DEPTH) ** 0.25
DEEPNORM_BETA = (8.0 * DEPTH) ** -0.25

kernel_name = "hymba_diffattn_retnet_ec_encoder"


def layer_norm(x, g, b):
    xf = x.astype(jnp.float32)
    mu = jnp.mean(xf, axis=-1, keepdims=True)
    var = jnp.mean(jnp.square(xf - mu), axis=-1, keepdims=True)
    y = (xf - mu) * lax.rsqrt(var + NORM_EPS) * g.astype(jnp.float32) + b.astype(jnp.float32)
    return y.astype(x.dtype)


def apply_rotary(x):
    S, d = x.shape[1], x.shape[-1]
    inv_freq = 1.0 / (ROPE_THETA ** (jnp.arange(0, d, 2, dtype=jnp.float32) / d))
    ang = jnp.arange(S, dtype=jnp.float32)[:, None] * inv_freq[None, :]
    shape = (1, S) + (1,) * (x.ndim - 3) + (d // 2,)
    cos = jnp.cos(ang).reshape(shape).astype(x.dtype)
    sin = jnp.sin(ang).reshape(shape).astype(x.dtype)
    x1, x2 = x[..., : d // 2], x[..., d // 2:]
    return jnp.concatenate([x1 * cos - x2 * sin, x2 * cos + x1 * sin], axis=-1)


def diff_attention(q, k, v, lam):
    Bsz, S, H, _, d = q.shape
    nq = S // Q_BLOCK
    scale = d ** -0.5
    qb = q.reshape(Bsz, nq, Q_BLOCK, H, 2, d).transpose(1, 0, 2, 3, 4, 5)

    def one_block(qi):
        s = jnp.einsum('bqhcd,bkhcd->bhcqk', qi, k).astype(jnp.float32) * scale
        p = jax.nn.softmax(s, axis=-1)
        a = p[:, :, 0] - lam * p[:, :, 1]
        return jnp.einsum('bhqk,bkhv->bqhv', a.astype(v.dtype), v)

    o = lax.map(one_block, qb)
    return o.transpose(1, 0, 2, 3, 4).reshape(Bsz, S, H, v.shape[-1])


def retention_direction(q, k, v, log_gamma, strict):
    Bsz, S, H, dk = q.shape
    dv = v.shape[-1]
    n = S // RET_CHUNK

    def chunks(t):
        return t.reshape(Bsz, n, RET_CHUNK, H, t.shape[-1]).transpose(1, 0, 3, 2, 4)

    pos = jnp.arange(RET_CHUNK, dtype=jnp.float32)
    offs = pos[:, None] - pos[None, :]
    mask = (offs > 0) if strict else (offs >= 0)
    intra = jnp.where(mask[None], jnp.exp(jnp.where(mask, offs, 0.0)[None] * log_gamma[:, None, None]), 0.0)
    xi = jnp.exp((pos + 1.0)[None, :] * log_gamma[:, None])
    zeta = jnp.exp((RET_CHUNK - 1.0 - pos)[None, :] * log_gamma[:, None])
    carry_decay = jnp.exp(RET_CHUNK * log_gamma)[:, None, None]

    def step(state, blk):
        qi, ki, vi = blk
        s = jnp.einsum('bhcd,bhed->bhce', qi, ki) * intra
        o = (jnp.einsum('bhce,bhev->bhcv', s, vi)
             + jnp.einsum('bhcd,bhdv->bhcv', qi, state) * xi[..., None])
        state = carry_decay * state + jnp.einsum('bhcd,bhcv->bhdv', ki * zeta[..., None], vi)
        return state, o

    state0 = jnp.zeros((Bsz, H, dk, dv), jnp.float32)
    _, o = lax.scan(step, state0, (chunks(q), chunks(k), chunks(v)))
    return o.transpose(1, 0, 3, 2, 4).reshape(Bsz, S, H, dv)


def bidirectional_retention(q, k, v, log2_rate):
    log_gamma = jnp.log1p(-jnp.exp2(log2_rate.astype(jnp.float32)))
    fwd = retention_direction(q, k, v, log_gamma[0], strict=False)
    bwd = retention_direction(q[:, ::-1], k[:, ::-1], v[:, ::-1], log_gamma[1], strict=True)[:, ::-1]
    return fwd + bwd


def expert_choice_ffn(x, w_router, w_gate, w_up, w_down):
    Bsz, S, D = x.shape
    T = Bsz * S
    C = min(T, max(1, EC_CAPACITY * T // N_EXPERTS))
    xt = x.reshape(T, D)
    aff = jax.nn.softmax((xt @ w_router).astype(jnp.float32), axis=-1)
    g, idx = lax.top_k(aff.T, C)
    xe = xt[idx]
    hdn = jax.nn.silu(jnp.einsum('ecd,edf->ecf', xe, w_gate)) * jnp.einsum('ecd,edf->ecf', xe, w_up)
    ye = jnp.einsum('ecf,efd->ecd', hdn, w_down) * g[..., None].astype(x.dtype)
    y = jnp.zeros_like(xt).at[idx.reshape(-1)].add(ye.reshape(-1, D).astype(xt.dtype))
    return y.reshape(Bsz, S, D)


def encoder_layer(x, layer_idx, w_in, w_out, diff_lambda, diff_norm_g, ret_log2_decay, ret_norm_g,
                  ln_g, ln_b, w_router, w_gate, w_up, w_down):
    Bsz, S, _ = x.shape
    h = x @ w_in
    cuts = list(np.cumsum(IN_SPLITS)[:-1])
    qd, kd, vd, qr, kr, vr, gr = jnp.split(h, cuts, axis=-1)

    lambda_init = 0.8 - 0.6 * math.exp(-0.3 * layer_idx)
    lamf = diff_lambda.astype(jnp.float32)
    lam = jnp.exp(jnp.sum(lamf[0] * lamf[1])) - jnp.exp(jnp.sum(lamf[2] * lamf[3])) + lambda_init
    qd = apply_rotary(qd.reshape(Bsz, S, H_DIFF, 2, D_DIFF))
    kd = apply_rotary(kd.reshape(Bsz, S, H_DIFF, 2, D_DIFF))
    vd = vd.reshape(Bsz, S, H_DIFF, 2 * D_DIFF)
    od = diff_attention(qd, kd, vd, lam).astype(jnp.float32)
    od = od * lax.rsqrt(jnp.mean(jnp.square(od), axis=-1, keepdims=True) + NORM_EPS)
    od = od * diff_norm_g.astype(jnp.float32) * (1.0 - lambda_init)
    od = od.reshape(Bsz, S, DIFF_WIDTH).astype(x.dtype)

    qr = apply_rotary(qr.reshape(Bsz, S, H_RET, DK_RET)).astype(jnp.float32)
    kr = (apply_rotary(kr.reshape(Bsz, S, H_RET, DK_RET)).astype(jnp.float32)) * (DK_RET ** -0.5)
    vr = vr.reshape(Bsz, S, H_RET, DV_RET).astype(jnp.float32)
    orr = bidirectional_retention(qr, kr, vr, ret_log2_decay)
    mu = jnp.mean(orr, axis=-1, keepdims=True)
    var = jnp.mean(jnp.square(orr - mu), axis=-1, keepdims=True)
    orr = (orr - mu) * lax.rsqrt(var + NORM_EPS) * ret_norm_g.astype(jnp.float32).reshape(H_RET, DV_RET)
    orr = orr.reshape(Bsz, S, RET_WIDTH).astype(x.dtype) * jax.nn.silu(gr)

    mix = jnp.concatenate([od, orr], axis=-1) @ w_out
    x = layer_norm(DEEPNORM_ALPHA * x + mix, ln_g[0], ln_b[0])
    x = layer_norm(DEEPNORM_ALPHA * x + expert_choice_ffn(x, w_router, w_gate, w_up, w_down), ln_g[1], ln_b[1])
    return x


def setup_inputs(seed: int = 0) -> dict:
    key = jax.random.key(seed)
    ks = jax.random.split(key, 16)
    f32 = jnp.float32
    x_prompt = jax.random.normal(ks[0], (BATCH, SEQ, D_MODEL), f32)
    x_sample = jax.random.normal(ks[1], (DEC_BATCH, DEC_SEQ, D_MODEL), f32)
    col_scale = jnp.concatenate([
        jnp.full((w,), DEEPNORM_BETA if is_v else 1.0, f32) for w, is_v in zip(IN_SPLITS, IN_IS_VALUE)])
    w_in = jax.random.normal(ks[2], (DEPTH, D_MODEL, IN_WIDTH), f32) * (D_MODEL ** -0.5) * col_scale
    w_out = jax.random.normal(ks[3], (DEPTH, MIX_WIDTH, D_MODEL), f32) * (MIX_WIDTH ** -0.5) * DEEPNORM_BETA
    diff_lambda = 0.1 * jax.random.normal(ks[4], (DEPTH, 4, D_DIFF), f32)
    diff_norm_g = 1.0 + 0.02 * jax.random.normal(ks[5], (DEPTH, 2 * D_DIFF), f32)
    base_rate = -(5.0 + jnp.arange(H_RET, dtype=f32))
    ret_log2_decay = base_rate[None, None, :] + 0.1 * jax.random.normal(ks[6], (DEPTH, 2, H_RET), f32)
    ret_norm_g = 1.0 + 0.02 * jax.random.normal(ks[7], (DEPTH, RET_WIDTH), f32)
    ln_g = 1.0 + 0.02 * jax.random.normal(ks[8], (DEPTH, 2, D_MODEL), f32)
    ln_b = 0.02 * jax.random.normal(ks[9], (DEPTH, 2, D_MODEL), f32)
    w_router = jax.random.normal(ks[10], (DEPTH, D_MODEL, N_EXPERTS), f32) * (D_MODEL ** -0.5)
    w_gate = jax.random.normal(ks[11], (DEPTH, N_EXPERTS, D_MODEL, D_FF_EXPERT), f32) * (D_MODEL ** -0.5)
    w_up = jax.random.normal(ks[12], (DEPTH, N_EXPERTS, D_MODEL, D_FF_EXPERT), f32) * (D_MODEL ** -0.5)
    w_down = jax.random.normal(ks[13], (DEPTH, N_EXPERTS, D_FF_EXPERT, D_MODEL), f32) * (D_FF_EXPERT ** -0.5) * DEEPNORM_BETA
    return {"x_prompt": x_prompt, "x_sample": x_sample, "w_in": w_in, "w_out": w_out,
            "diff_lambda": diff_lambda, "diff_norm_g": diff_norm_g, "ret_log2_decay": ret_log2_decay,
            "ret_norm_g": ret_norm_g, "ln_g": ln_g, "ln_b": ln_b, "w_router": w_router,
            "w_gate": w_gate, "w_up": w_up, "w_down": w_down}


def reference(x_prompt, x_sample, w_in, w_out, diff_lambda, diff_norm_g, ret_log2_decay, ret_norm_g,
              ln_g, ln_b, w_router, w_gate, w_up, w_down):
    def trunk(x):
        for l in range(DEPTH):
            x = encoder_layer(x, l, w_in[l], w_out[l], diff_lambda[l], diff_norm_g[l], ret_log2_decay[l],
                              ret_norm_g[l], ln_g[l], ln_b[l], w_router[l], w_gate[l], w_up[l], w_down[l])
        return x

    y_prompt = trunk(x_prompt)
    y_sample = trunk(x_sample)
    return (y_prompt, y_sample)
```

```python
import functools
import math

import jax
import jax.numpy as jnp
from jax import lax
from jax.experimental import pallas as pl
from jax.experimental.pallas import tpu as pltpu

F32 = jnp.float32
BF16 = jnp.bfloat16
I32 = jnp.int32

H_DIFF = 8
D_DIFF = 128
H_RET = 8
DK_RET = 256
DV_RET = 256
HEAD_W = 256
GROUP_W = H_DIFF * HEAD_W
N_EXPERTS = 16
EC_CAPACITY = 2
ROPE_THETA = 10000.0
NORM_EPS = 1e-5
DEPTH = 1
DEEPNORM_ALPHA = (2.0 * DEPTH) ** 0.25

LANES = 128
V7X_VMEM_LIMIT_BYTES = 56 * 2**20


def _tile(dim, pref):
    t = min(dim, pref)
    while dim % t:
        t //= 2
    return t


def _params(sem):
    return pltpu.CompilerParams(dimension_semantics=sem, vmem_limit_bytes=V7X_VMEM_LIMIT_BYTES)


def _sigmoid(x):
    return 1.0 / (1.0 + jnp.exp(-x))


def _proj_kernel(x_ref, w_ref, *rest, kind, n_q_blocks):
    if kind == "plain":
        (o_ref,) = rest
    else:
        cos_ref, sin_ref, o_ref = rest
    acc = jnp.dot(x_ref[...], w_ref[...], preferred_element_type=F32)
    tn = acc.shape[1]
    if kind == "plain":
        o_ref[...] = acc.astype(o_ref.dtype)
    elif kind == "rot128":
        cos = cos_ref[...]
        sin = sin_ref[...]
        for c in range(tn // LANES):
            xc = acc[:, c * LANES:(c + 1) * LANES]
            o_ref[:, c * LANES:(c + 1) * LANES] = (xc * cos + pltpu.roll(xc, 64, 1) * sin).astype(o_ref.dtype)
    else:
        cos = cos_ref[...]
        sin = sin_ref[...]
        scale = jnp.where(pl.program_id(1) >= n_q_blocks, DK_RET ** -0.5, 1.0).astype(F32)
        for c in range(tn // HEAD_W):
            x1 = acc[:, c * HEAD_W:c * HEAD_W + LANES]
            x2 = acc[:, c * HEAD_W + LANES:(c + 1) * HEAD_W]
            o_ref[:, c * HEAD_W:c * HEAD_W + LANES] = ((x1 * cos - x2 * sin) * scale).astype(o_ref.dtype)
            o_ref[:, c * HEAD_W + LANES:(c + 1) * HEAD_W] = ((x2 * cos + x1 * sin) * scale).astype(o_ref.dtype)


def _project(xb, wb, col0, ncols, kind, out_dtype, seq, tables=None):
    T, D = xb.shape
    tm = _tile(seq, 1024)
    tn = _tile(ncols, 1024)
    nb0 = col0 // tn
    in_specs = [pl.BlockSpec((tm, D), lambda i, j: (i, 0)),
                pl.BlockSpec((D, tn), lambda i, j: (0, j + nb0))]
    args = [xb, wb]
    if kind != "plain":
        npos = seq // tm
        in_specs += [pl.BlockSpec((tm, LANES), lambda i, j: (i % npos, 0))] * 2
        args += list(tables)
    return pl.pallas_call(
        functools.partial(_proj_kernel, kind=kind, n_q_blocks=GROUP_W // tn),
        grid=(T // tm, ncols // tn),
        in_specs=in_specs,
        out_specs=pl.BlockSpec((tm, tn), lambda i, j: (i, j)),
        out_shape=jax.ShapeDtypeStruct((T, ncols), out_dtype),
        compiler_params=_params(("parallel", "arbitrary")),
        name="in_proj_" + kind,
    )(*args)


def _rope_tables(seq, d):
    inv_freq = 1.0 / (ROPE_THETA ** (jnp.arange(0, d, 2, dtype=F32) / d))
    ang = jnp.arange(seq, dtype=F32)[:, None] * inv_freq[None, :]
    return jnp.cos(ang), jnp.sin(ang)


def _attn_kernel(q_ref, k_ref, v_ref, lam_ref, g_ref, o_ref, m_ref, l_ref, acc_ref, *, tk, lam_init):
    seq = k_ref.shape[0]
    tq = q_ref.shape[0]
    cs = (D_DIFF ** -0.5) * math.log2(math.e)
    q = q_ref[...]
    qs = (q[:, :D_DIFF], q[:, D_DIFF:])
    m_ref[...] = jnp.full(m_ref.shape, -jnp.inf, F32)
    l_ref[...] = jnp.zeros(l_ref.shape, F32)
    acc_ref[...] = jnp.zeros(acc_ref.shape, F32)

    def body(i, carry):
        off = pl.multiple_of(i * tk, tk)
        k = k_ref[pl.ds(off, tk), :]
        v = v_ref[pl.ds(off, tk), :]
        for c in range(2):
            s = lax.dot_general(qs[c], k[:, c * D_DIFF:(c + 1) * D_DIFF], (((1,), (1,)), ((), ())),
                                preferred_element_type=F32)
            m_old = m_ref[c]
            m_new = jnp.maximum(m_old, jnp.max(s, axis=-1, keepdims=True))
            alpha = jnp.exp2((m_old - m_new) * cs)
            p = jnp.exp2((s - m_new) * cs)
            l_ref[c] = alpha * l_ref[c] + jnp.sum(p, axis=-1, keepdims=True)
            acc_ref[c] = alpha * acc_ref[c] + jnp.dot(p.astype(BF16), v, preferred_element_type=F32)
            m_ref[c] = m_new
        return carry

    lax.fori_loop(0, seq // tk, body, 0)

    dl = lam_ref[...]
    lam = (jnp.exp(jnp.sum(dl[0:1] * dl[1:2], axis=-1, keepdims=True))
           - jnp.exp(jnp.sum(dl[2:3] * dl[3:4], axis=-1, keepdims=True)) + lam_init)
    o = acc_ref[0] / l_ref[0] - lam * (acc_ref[1] / l_ref[1])
    o = o * lax.rsqrt(jnp.mean(o * o, axis=-1, keepdims=True) + NORM_EPS)
    o_ref[...] = (o * g_ref[...] * (1.0 - lam_init)).astype(o_ref.dtype)


def _diff_attention(qk, v, diff_lambda, diff_norm_g, seq, lam_init):
    T = v.shape[0]
    nb = T // seq
    tq = _tile(seq, 1024)
    tk = _tile(seq, 1024)
    nq = seq // tq
    return pl.pallas_call(
        functools.partial(_attn_kernel, tk=tk, lam_init=lam_init),
        grid=(nb, H_DIFF, nq),
        in_specs=[pl.BlockSpec((tq, HEAD_W), lambda b, h, i: (b * nq + i, h)),
                  pl.BlockSpec((seq, HEAD_W), lambda b, h, i: (b, H_DIFF + h)),
                  pl.BlockSpec((seq, HEAD_W), lambda b, h, i: (b, h)),
                  pl.BlockSpec((4, D_DIFF), lambda b, h, i: (0, 0)),
                  pl.BlockSpec((1, HEAD_W), lambda b, h, i: (0, 0))],
        out_specs=pl.BlockSpec((tq, HEAD_W), lambda b, h, i: (b * nq + i, h)),
        out_shape=jax.ShapeDtypeStruct((T, GROUP_W), BF16),
        scratch_shapes=[pltpu.VMEM((2, tq, 1), F32), pltpu.VMEM((2, tq, 1), F32),
                        pltpu.VMEM((2, tq, HEAD_W), F32)],
        compiler_params=_params(("parallel", "parallel", "arbitrary")),
        name="diff_attention",
    )(qk, qk, v, diff_lambda, diff_norm_g.reshape(1, HEAD_W))


def _ret_kernel(rate_ref, qf_ref, kf_ref, vf_ref, gf_ref, qb_ref, kb_ref, vb_ref, gb_ref, gn_ref,
                o_ref, sf_ref, sb_ref, osum_ref, *, cc, n):
    h = pl.program_id(0) % H_RET
    c = pl.program_id(1)

    @pl.when(c == 0)
    def _():
        sf_ref[...] = jnp.zeros(sf_ref.shape, F32)
        sb_ref[...] = jnp.zeros(sb_ref.shape, F32)

    def log_gamma(d):
        rate = jnp.full((1, 1), rate_ref[d, h], F32)
        return jnp.log1p(-jnp.exp2(rate))

    lgf = log_gamma(0)
    lgb = log_gamma(1)
    ii = lax.broadcasted_iota(I32, (cc, cc), 0).astype(F32)
    jj = lax.broadcasted_iota(I32, (cc, cc), 1).astype(F32)
    dlt = ii - jj
    dec_f = jnp.where(dlt >= 0, jnp.exp(jnp.where(dlt >= 0, dlt, 0.0) * lgf), 0.0)
    dec_b = jnp.where(dlt < 0, jnp.exp(jnp.where(dlt < 0, -dlt, 0.0) * lgb), 0.0)
    pos = lax.broadcasted_iota(I32, (cc, 1), 0).astype(F32)

    def direction(q_ref, k_ref, v_ref, s_ref, dec, xi, zeta, carry):
        q = q_ref[...]
        k = k_ref[...]
        v = v_ref[...]
        s = lax.dot_general(q, k, (((1,), (1,)), ((), ())), preferred_element_type=F32) * dec
        state = s_ref[...]
        o = (jnp.dot(s.astype(BF16), v, preferred_element_type=F32)
             + jnp.dot(q, state.astype(BF16), preferred_element_type=F32) * xi)
        kz = (k.astype(F32) * zeta).astype(BF16)
        s_ref[...] = carry * state + lax.dot_general(kz, v, (((0,), (0,)), ((), ())),
                                                     preferred_element_type=F32)
        return o

    o_f = direction(qf_ref, kf_ref, vf_ref, sf_ref, dec_f,
                    jnp.exp((pos + 1.0) * lgf), jnp.exp((cc - 1.0 - pos) * lgf), jnp.exp(cc * lgf))
    o_b = direction(qb_ref, kb_ref, vb_ref, sb_ref, dec_b,
                    jnp.exp((cc - pos) * lgb), jnp.exp(pos * lgb), jnp.exp(cc * lgb))

    rows_f = pl.ds(pl.multiple_of(c * cc, cc), cc)
    rows_b = pl.ds(pl.multiple_of((n - 1 - c) * cc, cc), cc)

    @pl.when(c < n // 2)
    def _():
        osum_ref[rows_f, :] = o_f
        osum_ref[rows_b, :] = o_b

    def finish(o, gate_ref):
        mu = jnp.mean(o, axis=-1, keepdims=True)
        var = jnp.mean(jnp.square(o - mu), axis=-1, keepdims=True)
        y = (o - mu) * lax.rsqrt(var + NORM_EPS) * gn_ref[...]
        gate = gate_ref[...]
        return (y * (gate * _sigmoid(gate))).astype(o_ref.dtype)

    @pl.when(c >= n // 2)
    def _():
        o_ref[rows_f, :] = finish(osum_ref[rows_f, :] + o_f, gf_ref)
        o_ref[rows_b, :] = finish(osum_ref[rows_b, :] + o_b, gb_ref)


def _retention(qk, v, gate, ret_log2_decay, ret_norm_g, seq):
    T = v.shape[0]
    nb = T // seq
    cc = _tile(seq // 2, 256)
    n = seq // cc
    assert n % 2 == 0

    def fwd(col):
        return pl.BlockSpec((cc, HEAD_W), lambda bh, c: ((bh // H_RET) * n + c, (bh % H_RET) + col))

    def bwd(col):
        return pl.BlockSpec((cc, HEAD_W), lambda bh, c: ((bh // H_RET) * n + n - 1 - c, (bh % H_RET) + col))

    return pl.pallas_call(
        functools.partial(_ret_kernel, cc=cc, n=n),
        grid=(nb * H_RET, n),
        in_specs=[pl.BlockSpec(memory_space=pltpu.SMEM),
                  fwd(0), fwd(H_RET), fwd(0), fwd(0), bwd(0), bwd(H_RET), bwd(0), bwd(0),
                  pl.BlockSpec((1, HEAD_W), lambda bh, c: (0, bh % H_RET))],
        out_specs=pl.BlockSpec((seq, HEAD_W), lambda bh, c: (bh // H_RET, bh % H_RET)),
        out_shape=jax.ShapeDtypeStruct((T, GROUP_W), BF16),
        scratch_shapes=[pltpu.VMEM((DK_RET, DV_RET), F32), pltpu.VMEM((DK_RET, DV_RET), F32),
                        pltpu.VMEM((seq, HEAD_W), F32)],
        compiler_params=_params(("parallel", "arbitrary")),
        name="retention",
    )(ret_log2_decay, qk, qk, v, gate, qk, qk, v, gate, ret_norm_g.reshape(1, GROUP_W))


def _outproj_kernel(a1_ref, a2_ref, w1_ref, w2_ref, o_ref):
    o_ref[...] = (jnp.dot(a1_ref[...], w1_ref[...], preferred_element_type=F32)
                  + jnp.dot(a2_ref[...], w2_ref[...], preferred_element_type=F32))


def _out_projection(od, orr, wob):
    T = od.shape[0]
    D = wob.shape[1]
    tm = _tile(T, 1024)
    tn = _tile(D, 1024)
    return pl.pallas_call(
        _outproj_kernel,
        grid=(T // tm, D // tn),
        in_specs=[pl.BlockSpec((tm, GROUP_W), lambda i, j: (i, 0)),
                  pl.BlockSpec((tm, GROUP_W), lambda i, j: (i, 0)),
                  pl.BlockSpec((GROUP_W, tn), lambda i, j: (0, j)),
                  pl.BlockSpec((GROUP_W, tn), lambda i, j: (1, j))],
        out_specs=pl.BlockSpec((tm, tn), lambda i, j: (i, j)),
        out_shape=jax.ShapeDtypeStruct((T, D), F32),
        compiler_params=_params(("parallel", "arbitrary")),
        name="out_proj",
    )(od, orr, wob, wob)


def _layer_norm_rows(z, g, b):
    mu = jnp.mean(z, axis=-1, keepdims=True)
    var = jnp.mean(jnp.square(z - mu), axis=-1, keepdims=True)
    return (z - mu) * lax.rsqrt(var + NORM_EPS) * g + b


def _ln1_kernel(x_ref, mix_ref, g_ref, b_ref, wr_ref, x1_ref, lg_ref):
    y = _layer_norm_rows(DEEPNORM_ALPHA * x_ref[...] + mix_ref[...], g_ref[...], b_ref[...])
    x1_ref[...] = y
    lg_ref[...] = lax.dot_general(wr_ref[...], y, (((1,), (1,)), ((), ())),
                                  precision=lax.Precision.HIGHEST, preferred_element_type=F32)


def _ln1_router(x, mix, g, b, w_router_t):
    T, D = x.shape
    tr = _tile(T, 256)
    return pl.pallas_call(
        _ln1_kernel,
        grid=(T // tr,),
        in_specs=[pl.BlockSpec((tr, D), lambda i: (i, 0)),
                  pl.BlockSpec((tr, D), lambda i: (i, 0)),
                  pl.BlockSpec((1, D), lambda i: (0, 0)),
                  pl.BlockSpec((1, D), lambda i: (0, 0)),
                  pl.BlockSpec((N_EXPERTS, D), lambda i: (0, 0))],
        out_specs=[pl.BlockSpec((tr, D), lambda i: (i, 0)),
                   pl.BlockSpec((N_EXPERTS, tr), lambda i: (0, i))],
        out_shape=[jax.ShapeDtypeStruct((T, D), F32), jax.ShapeDtypeStruct((N_EXPERTS, T), F32)],
        compiler_params=_params(("parallel",)),
        name="ln1_router",
    )(x, mix, g.reshape(1, D), b.reshape(1, D), w_router_t)


def _route_kernel(lg_ref, idx_ref, gm_ref, pos_ref, *, cap, tok0):
    E, R, _ = lg_ref.shape
    logits = lg_ref[...]
    ex = jnp.exp(logits - jnp.max(logits, axis=0, keepdims=True))
    aff = ex / jnp.sum(ex, axis=0, keepdims=True)
    bits = pltpu.bitcast(aff, I32)

    def count(pred):
        c = jnp.sum(jnp.where(pred, 1.0, 0.0), axis=2, keepdims=True)
        return jnp.sum(c, axis=1, keepdims=True)

    capf = jnp.float32(cap)

    def thr_step(i, thr):
        cand = thr | lax.shift_left(jnp.int32(1), 30 - i)
        return jnp.where(count(bits >= cand) >= capf, cand, thr)

    thr = lax.fori_loop(0, 31, thr_step, jnp.zeros((E, 1, 1), I32))
    gt = bits > thr
    eq = bits == thr
    need = capf - count(gt)
    tid = (lax.broadcasted_iota(I32, (E, R, LANES), 1) * LANES
           + lax.broadcasted_iota(I32, (E, R, LANES), 2))
    nbits = int(math.log2(R * LANES))

    def tie_step(i, v):
        cand = v | lax.shift_left(jnp.int32(1), nbits - 1 - i)
        return jnp.where(count(eq & (tid < cand)) < need, cand, v)

    vtie = lax.fori_loop(0, nbits, tie_step, jnp.zeros((E, 1, 1), I32))
    mask = jnp.where(gt | (eq & (tid <= vtie)), 1.0, 0.0)
    gm_ref[...] = aff * mask

    mb = mask.astype(BF16)
    li = lax.broadcasted_iota(I32, (LANES, LANES), 0)
    lj = lax.broadcasted_iota(I32, (LANES, LANES), 1)
    upper = jnp.where(li <= lj, 1.0, 0.0).astype(BF16)
    lower = jnp.where(li >= lj, 1.0, 0.0).astype(BF16)
    ri = lax.broadcasted_iota(I32, (R, R), 0)
    rj = lax.broadcasted_iota(I32, (R, R), 1)
    strict = jnp.where(rj < ri, 1.0, 0.0).astype(BF16)
    slot = lax.broadcasted_iota(I32, (1, cap), 1).astype(F32)
    rcol = lax.broadcasted_iota(I32, (R, 1), 0).astype(F32)
    for e in range(E):
        me = mb[e]
        winc = jnp.dot(me, upper, preferred_element_type=F32)
        rows = winc[:, LANES - 1:LANES]
        offs = jnp.sum(jnp.dot(strict, me, preferred_element_type=F32), axis=-1, keepdims=True)
        pos_ref[e] = (offs + winc - mask[e]).astype(I32)
        ends = offs + rows
        rs = jnp.sum(jnp.where(ends <= slot, 1.0, 0.0), axis=0, keepdims=True)
        onehot = rcol == rs
        offs_s = jnp.sum(jnp.where(onehot, offs, 0.0), axis=0, keepdims=True)
        winc_t = lax.dot_general(lower, me, (((1,), (1,)), ((), ())), preferred_element_type=F32)
        wg = jnp.dot(winc_t.astype(BF16), jnp.where(onehot, 1.0, 0.0).astype(BF16),
                     preferred_element_type=F32)
        ls = jnp.sum(jnp.where(wg <= slot - offs_s, 1.0, 0.0), axis=0, keepdims=True)
        idx_ref[e:e + 1, :] = (rs * LANES + ls).astype(I32) + tok0


def _route(logits_t, cap, tok0):
    E, Tg = logits_t.shape
    R = Tg // LANES
    idx, gm, pos = pl.pallas_call(
        functools.partial(_route_kernel, cap=cap, tok0=tok0),
        out_shape=[jax.ShapeDtypeStruct((E, cap), I32), jax.ShapeDtypeStruct((E, R, LANES), F32),
                   jax.ShapeDtypeStruct((E, R, LANES), I32)],
        compiler_params=pltpu.CompilerParams(vmem_limit_bytes=V7X_VMEM_LIMIT_BYTES),
        name="route",
    )(logits_t.reshape(E, R, LANES))
    return idx, gm.reshape(E, Tg), pos.reshape(E, Tg)


def _gather_kernel(idx_ref, x_hbm, o_ref, buf, sem):
    tg = buf.shape[0]

    def row_copy(src_row, dst_row):
        return pltpu.make_async_copy(x_hbm.at[pl.ds(src_row, 1)], buf.at[pl.ds(dst_row, 1)], sem)

    def issue(r, carry):
        row_copy(idx_ref[0, 0, r], r).start()
        return carry

    def drain(r, carry):
        row_copy(0, 0).wait()
        return carry

    lax.fori_loop(0, tg, issue, 0)
    lax.fori_loop(0, tg, drain, 0)
    o_ref[...] = buf[...].astype(o_ref.dtype)


def _gather_tokens(x1, idx_all):
    T, D = x1.shape
    n_rows = idx_all.size
    tg = _tile(n_rows, 256)
    n_tiles = n_rows // tg
    return pl.pallas_call(
        _gather_kernel,
        grid=(n_tiles,),
        in_specs=[pl.BlockSpec((1, 1, tg), lambda i: (i, 0, 0), memory_space=pltpu.SMEM),
                  pl.BlockSpec(memory_space=pl.ANY)],
        out_specs=pl.BlockSpec((tg, D), lambda i: (i, 0)),
        out_shape=jax.ShapeDtypeStruct((n_rows, D), BF16),
        scratch_shapes=[pltpu.VMEM((tg, D), F32), pltpu.SemaphoreType.DMA(())],
        compiler_params=_params(("arbitrary",)),
        name="gather_tokens",
    )(idx_all.reshape(n_tiles, 1, tg), x1)


def _ffn_a_kernel(x_ref, wg_ref, wu_ref, h_ref):
    x = x_ref[...]
    g = jnp.dot(x, wg_ref[...].astype(BF16), preferred_element_type=F32)
    u = jnp.dot(x, wu_ref[...].astype(BF16), preferred_element_type=F32)
    h_ref[...] = (g * _sigmoid(g) * u).astype(h_ref.dtype)


def _ffn_hidden(xe, w_gate, w_up):
    E, Ctot, D = xe.shape
    F = w_gate.shape[2]
    tm = _tile(Ctot, 1536)
    tf = _tile(F, 256)
    return pl.pallas_call(
        _ffn_a_kernel,
        grid=(E, Ctot // tm, F // tf),
        in_specs=[pl.BlockSpec((None, tm, D), lambda e, m, f: (e, m, 0)),
                  pl.BlockSpec((None, D, tf), lambda e, m, f: (e, 0, f)),
                  pl.BlockSpec((None, D, tf), lambda e, m, f: (e, 0, f))],
        out_specs=pl.BlockSpec((None, tm, tf), lambda e, m, f: (e, m, f)),
        out_shape=jax.ShapeDtypeStruct((E, Ctot, F), BF16),
        compiler_params=_params(("parallel", "parallel", "arbitrary")),
        name="ffn_hidden",
    )(xe, w_gate, w_up)


def _ffn_b_kernel(h_ref, wd_ref, y_ref):
    k = pl.program_id(3)
    p = jnp.dot(h_ref[...], wd_ref[...].astype(BF16), preferred_element_type=F32)

    @pl.when(k == 0)
    def _():
        y_ref[...] = p

    @pl.when(k > 0)
    def _():
        y_ref[...] += p


def _ffn_out(h, w_down):
    E, Ctot, F = h.shape
    D = w_down.shape[2]
    tm = _tile(Ctot, 1536)
    tn = _tile(D, 1024)
    tk = _tile(F, 1024)
    return pl.pallas_call(
        _ffn_b_kernel,
        grid=(E, Ctot // tm, D // tn, F // tk),
        in_specs=[pl.BlockSpec((None, tm, tk), lambda e, m, n, k: (e, m, k)),
                  pl.BlockSpec((None, tk, tn), lambda e, m, n, k: (e, k, n))],
        out_specs=pl.BlockSpec((None, tm, tn), lambda e, m, n, k: (e, m, n)),
        out_shape=jax.ShapeDtypeStruct((E, Ctot, D), F32),
        compiler_params=_params(("parallel", "parallel", "parallel", "arbitrary")),
        name="ffn_out",
    )(h, w_down)


COMBINE_HALF = 64


def _combine_kernel(starts_ref, idx_ref, ye_hbm, gm_ref, x1_ref, g_ref, b_ref, o_ref, buf, sem,
                    *, row0, ctot):
    E = buf.shape[0]
    j = pl.program_id(0)
    tt = x1_ref.shape[0]
    gm = gm_ref[...]
    gt = jnp.concatenate([gm, jnp.zeros((LANES - E, tt), F32)], axis=0).T

    def row_copy(src_row, e, dst_row):
        return pltpu.make_async_copy(ye_hbm.at[pl.ds(src_row, 1)], buf.at[e, pl.ds(dst_row, 1)], sem)

    for half in range(tt // COMBINE_HALF):
        tile = j * (tt // COMBINE_HALF) + half
        t0 = j * tt + half * COMBINE_HALF
        buf[...] = jnp.zeros(buf.shape, F32)
        n_rows = jnp.int32(0)
        for e in range(E):
            s0 = starts_ref[e, tile]
            s1 = starts_ref[e, tile + 1]

            def issue(s, carry, e=e):
                row_copy(row0 + e * ctot + s, e, idx_ref[e, s] - t0).start()
                return carry

            lax.fori_loop(s0, s1, issue, 0)
            n_rows = n_rows + (s1 - s0)

        def drain(r, carry):
            row_copy(0, 0, 0).wait()
            return carry

        lax.fori_loop(0, n_rows, drain, 0)
        rows = slice(half * COMBINE_HALF, (half + 1) * COMBINE_HALF)
        y = jnp.zeros((COMBINE_HALF, buf.shape[2]), F32)
        for e in range(E):
            y = y + buf[e] * gt[rows, e:e + 1]
        z = DEEPNORM_ALPHA * x1_ref[rows, :] + y
        o_ref[rows, :] = _layer_norm_rows(z, g_ref[...], b_ref[...])


def _combine(ye_rows, starts, idx_local, gm, x1, g, b, tok0, row0, ctot):
    Tg = gm.shape[1]
    D = x1.shape[1]
    tt = 2 * COMBINE_HALF
    blk0 = tok0 // tt
    grid_spec = pltpu.PrefetchScalarGridSpec(
        num_scalar_prefetch=2,
        grid=(Tg // tt,),
        in_specs=[pl.BlockSpec(memory_space=pl.ANY),
                  pl.BlockSpec((N_EXPERTS, tt), lambda j, *_: (0, j)),
                  pl.BlockSpec((tt, D), lambda j, *_: (j + blk0, 0)),
                  pl.BlockSpec((1, D), lambda j, *_: (0, 0)),
                  pl.BlockSpec((1, D), lambda j, *_: (0, 0))],
        out_specs=pl.BlockSpec((tt, D), lambda j, *_: (j, 0)),
        scratch_shapes=[pltpu.VMEM((N_EXPERTS, COMBINE_HALF, D), F32), pltpu.SemaphoreType.DMA(())],
    )
    return pl.pallas_call(
        functools.partial(_combine_kernel, row0=row0, ctot=ctot),
        grid_spec=grid_spec,
        out_shape=jax.ShapeDtypeStruct((Tg, D), F32),
        compiler_params=_params(("arbitrary",)),
        name="combine_ln2",
    )(starts, idx_local, ye_rows, gm, x1, g.reshape(1, D), b.reshape(1, D))


def _encoder_layer(x_groups, layer_idx, w_in, w_out, diff_lambda, diff_norm_g, ret_log2_decay, ret_norm_g,
                   ln_g, ln_b, w_router, w_gate, w_up, w_down):
    seq = x_groups[0].shape[1]
    D = x_groups[0].shape[2]
    assert all(x.shape[1] == seq for x in x_groups)
    lam_init = 0.8 - 0.6 * math.exp(-0.3 * layer_idx)
    x = jnp.concatenate([g.reshape(-1, D) for g in x_groups], axis=0)
    xb = x.astype(BF16)
    wb = w_in.astype(BF16)
    wob = w_out.astype(BF16)

    c128, s128 = _rope_tables(seq, D_DIFF)
    c256, s256 = _rope_tables(seq, DK_RET)
    tab128 = (jnp.concatenate([c128, c128], axis=1), jnp.concatenate([-s128, s128], axis=1))
    tab256 = (c256, s256)

    qkd = _project(xb, wb, 0 * GROUP_W, 2 * GROUP_W, "rot128", BF16, seq, tab128)
    vd = _project(xb, wb, 2 * GROUP_W, GROUP_W, "plain", BF16, seq)
    qkr = _project(xb, wb, 3 * GROUP_W, 2 * GROUP_W, "rot256", BF16, seq, tab256)
    vr = _project(xb, wb, 5 * GROUP_W, GROUP_W, "plain", BF16, seq)
    gr = _project(xb, wb, 6 * GROUP_W, GROUP_W, "plain", F32, seq)

    od = _diff_attention(qkd, vd, diff_lambda, diff_norm_g, seq, lam_init)
    orr = _retention(qkr, vr, gr, ret_log2_decay, ret_norm_g, seq)
    mix = _out_projection(od, orr, wob)
    x1, logits_t = _ln1_router(x, mix, ln_g[0], ln_b[0], w_router.T)

    group_tokens = [g.shape[0] * seq for g in x_groups]
    caps = [min(t, max(1, EC_CAPACITY * t // N_EXPERTS)) for t in group_tokens]
    ctot = sum(caps)
    routed = []
    tok0 = 0
    for tg, cap in zip(group_tokens, caps):
        idx, gm, pos = _route(logits_t[:, tok0:tok0 + tg], cap, tok0)
        routed.append((idx, gm, pos))
        tok0 += tg
    idx_all = jnp.concatenate([r[0] for r in routed], axis=1)
    xe = _gather_tokens(x1, idx_all).reshape(N_EXPERTS, ctot, D)
    h = _ffn_hidden(xe, w_gate, w_up)
    ye = _ffn_out(h, w_down).reshape(N_EXPERTS * ctot, D)

    outs = []
    tok0 = 0
    row0 = 0
    for g, tg, cap, (idx, gm, pos) in zip(x_groups, group_tokens, caps, routed):
        starts = jnp.concatenate([pos[:, ::COMBINE_HALF], jnp.full((N_EXPERTS, 1), cap, I32)], axis=1)
        y = _combine(ye, starts, idx - tok0, gm, x1, ln_g[1], ln_b[1], tok0, row0, ctot)
        outs.append(y.reshape(g.shape))
        tok0 += tg
        row0 += cap
    return outs


def kernel(x_prompt, x_sample, w_in, w_out, diff_lambda, diff_norm_g, ret_log2_decay, ret_norm_g, ln_g, ln_b,
           w_router, w_gate, w_up, w_down):
    xs = [x_prompt, x_sample]
    for l in range(w_in.shape[0]):
        xs = _encoder_layer(xs, l, w_in[l], w_out[l], diff_lambda[l], diff_norm_g[l], ret_log2_decay[l],
                            ret_norm_g[l], ln_g[l], ln_b[l], w_router[l], w_gate[l], w_up[l], w_down[l])
    return (xs[0], xs[1])
```

```python
import functools
import math

import jax
import jax.numpy as jnp
from jax import lax
from jax.experimental import pallas as pl
from jax.experimental.pallas import tpu as pltpu

F32 = jnp.float32
BF16 = jnp.bfloat16
I32 = jnp.int32

H_DIFF = 8
D_DIFF = 128
H_RET = 8
DK_RET = 256
DV_RET = 256
HEAD_W = 256
GROUP_W = H_DIFF * HEAD_W
N_EXPERTS = 16
EC_CAPACITY = 2
ROPE_THETA = 10000.0
NORM_EPS = 1e-5
DEPTH = 1
DEEPNORM_ALPHA = (2.0 * DEPTH) ** 0.25
ATTN_Q_SCALE = (D_DIFF ** -0.5) * math.log2(math.e)

LANES = 128
V7X_VMEM_LIMIT_BYTES = 56 * 2**20
COMBINE_TOKENS = 64


def _tile(dim, pref):
    t = min(dim, pref)
    while dim % t:
        t //= 2
    return t


def _params(sem):
    return pltpu.CompilerParams(dimension_semantics=sem, vmem_limit_bytes=V7X_VMEM_LIMIT_BYTES)


def _sigmoid(x):
    return 1.0 / (1.0 + jnp.exp(-x))


def _proj_kernel(x_ref, w_ref, *rest, kind, n_q_blocks):
    if kind == "plain":
        (o_ref,) = rest
    else:
        cos_ref, sin_ref, o_ref = rest
    acc = jnp.dot(x_ref[...], w_ref[...], preferred_element_type=F32)
    tn = acc.shape[1]
    if kind == "plain":
        o_ref[...] = acc.astype(o_ref.dtype)
    elif kind == "rot128":
        scale = jnp.where(pl.program_id(1) < n_q_blocks, ATTN_Q_SCALE, 1.0).astype(F32)
        cos = cos_ref[...] * scale
        sin = sin_ref[...] * scale
        for c in range(tn // LANES):
            xc = acc[:, c * LANES:(c + 1) * LANES]
            o_ref[:, c * LANES:(c + 1) * LANES] = (xc * cos + pltpu.roll(xc, 64, 1) * sin).astype(o_ref.dtype)
    else:
        cos = cos_ref[...]
        sin = sin_ref[...]
        scale = jnp.where(pl.program_id(1) >= n_q_blocks, DK_RET ** -0.5, 1.0).astype(F32)
        for c in range(tn // HEAD_W):
            x1 = acc[:, c * HEAD_W:c * HEAD_W + LANES]
            x2 = acc[:, c * HEAD_W + LANES:(c + 1) * HEAD_W]
            o_ref[:, c * HEAD_W:c * HEAD_W + LANES] = ((x1 * cos - x2 * sin) * scale).astype(o_ref.dtype)
            o_ref[:, c * HEAD_W + LANES:(c + 1) * HEAD_W] = ((x2 * cos + x1 * sin) * scale).astype(o_ref.dtype)


def _project(xb, wb, col0, ncols, kind, out_dtype, seq, tables=None):
    T, D = xb.shape
    tm = _tile(seq, 1024)
    tn = _tile(ncols, 1024)
    nb0 = col0 // tn
    in_specs = [pl.BlockSpec((tm, D), lambda i, j: (i, 0)),
                pl.BlockSpec((D, tn), lambda i, j: (0, j + nb0))]
    args = [xb, wb]
    if kind != "plain":
        npos = seq // tm
        in_specs += [pl.BlockSpec((tm, LANES), lambda i, j: (i % npos, 0))] * 2
        args += list(tables)
    return pl.pallas_call(
        functools.partial(_proj_kernel, kind=kind, n_q_blocks=GROUP_W // tn),
        grid=(T // tm, ncols // tn),
        in_specs=in_specs,
        out_specs=pl.BlockSpec((tm, tn), lambda i, j: (i, j)),
        out_shape=jax.ShapeDtypeStruct((T, ncols), out_dtype),
        compiler_params=_params(("parallel", "arbitrary")),
        name="in_proj_" + kind,
    )(*args)


def _rope_tables(seq, d):
    inv_freq = 1.0 / (ROPE_THETA ** (jnp.arange(0, d, 2, dtype=F32) / d))
    ang = jnp.arange(seq, dtype=F32)[:, None] * inv_freq[None, :]
    return jnp.cos(ang), jnp.sin(ang)


ATTN_ROW_CHUNK = 16


def _attn_kernel(q_ref, k_ref, v_ref, lam_ref, g_ref, o_ref, m_ref, l_ref, mp_ref, acc_ref, s_ref, p_ref,
                 *, tk, lam_init):
    seq = k_ref.shape[0]
    tq = q_ref.shape[0]
    rc = min(ATTN_ROW_CHUNK, tq)
    m_ref[...] = jnp.full(m_ref.shape, -jnp.inf, F32)
    l_ref[...] = jnp.zeros(l_ref.shape, F32)
    acc_ref[...] = jnp.zeros(acc_ref.shape, F32)

    nt = tk // LANES

    def lane_fold(x, op):
        y = x[:, :LANES]
        for t in range(1, nt):
            y = op(y, x[:, t * LANES:(t + 1) * LANES])
        return y

    def body(i, carry):
        off = pl.multiple_of(i * tk, tk)
        for c in range(2):
            s_ref[c] = lax.dot_general(q_ref[:, c * D_DIFF:(c + 1) * D_DIFF],
                                       k_ref[pl.ds(off, tk), c * D_DIFF:(c + 1) * D_DIFF],
                                       (((1,), (1,)), ((), ())), preferred_element_type=F32)
        for c in range(2):
            for r in range(tq // rc):
                rows = slice(r * rc, (r + 1) * rc)
                mp_ref[rows, :] = lane_fold(s_ref[c, rows, :], jnp.maximum)
            m_old = m_ref[c]
            m_new = jnp.maximum(m_old, jnp.max(mp_ref[...], axis=-1, keepdims=True))
            alpha = jnp.exp2(m_old - m_new)
            m_ref[c] = m_new
            for r in range(tq // rc):
                rows = slice(r * rc, (r + 1) * rc)
                mrow = m_ref[c, rows, :]
                p = jnp.exp2(s_ref[c, rows, :] - jnp.concatenate([mrow] * nt, axis=1))
                mp_ref[rows, :] = lane_fold(p, jnp.add)
                p_ref[c, rows, :] = p.astype(BF16)
            l_ref[c] = alpha * l_ref[c] + mp_ref[...]
            acc_ref[c] = (jnp.concatenate([alpha] * (HEAD_W // LANES), axis=1) * acc_ref[c]
                          + jnp.dot(p_ref[c], v_ref[pl.ds(off, tk), :], preferred_element_type=F32))
        return carry

    lax.fori_loop(0, seq // tk, body, 0)

    dl = lam_ref[...]
    lam = (jnp.exp(jnp.sum(dl[0:1] * dl[1:2], axis=-1, keepdims=True))
           - jnp.exp(jnp.sum(dl[2:3] * dl[3:4], axis=-1, keepdims=True)) + lam_init)
    l0 = jnp.sum(l_ref[0], axis=-1, keepdims=True)
    l1 = jnp.sum(l_ref[1], axis=-1, keepdims=True)
    o = acc_ref[0] / l0 - lam * (acc_ref[1] / l1)
    o = o * lax.rsqrt(jnp.mean(o * o, axis=-1, keepdims=True) + NORM_EPS)
    o_ref[...] = (o * g_ref[...] * (1.0 - lam_init)).astype(o_ref.dtype)


def _diff_attention(qk, v, diff_lambda, diff_norm_g, seq, lam_init):
    T = v.shape[0]
    nb = T // seq
    tq = _tile(seq, 1024)
    tk = _tile(seq, 1024)
    nq = seq // tq
    return pl.pallas_call(
        functools.partial(_attn_kernel, tk=tk, lam_init=lam_init),
        grid=(nb, H_DIFF, nq),
        in_specs=[pl.BlockSpec((tq, HEAD_W), lambda b, h, i: (b * nq + i, h)),
                  pl.BlockSpec((seq, HEAD_W), lambda b, h, i: (b, H_DIFF + h)),
                  pl.BlockSpec((seq, HEAD_W), lambda b, h, i: (b, h)),
                  pl.BlockSpec((4, D_DIFF), lambda b, h, i: (0, 0)),
                  pl.BlockSpec((1, HEAD_W), lambda b, h, i: (0, 0))],
        out_specs=pl.BlockSpec((tq, HEAD_W), lambda b, h, i: (b * nq + i, h)),
        out_shape=jax.ShapeDtypeStruct((T, GROUP_W), BF16),
        scratch_shapes=[pltpu.VMEM((2, tq, LANES), F32), pltpu.VMEM((2, tq, LANES), F32),
                        pltpu.VMEM((tq, LANES), F32),
                        pltpu.VMEM((2, tq, HEAD_W), F32), pltpu.VMEM((2, tq, tk), F32),
                        pltpu.VMEM((2, tq, tk), BF16)],
        compiler_params=_params(("parallel", "parallel", "arbitrary")),
        name="diff_attention",
    )(qk, qk, v, diff_lambda, diff_norm_g.reshape(1, HEAD_W))


def _ret_kernel(rate_ref, qf_ref, kf_ref, vf_ref, gf_ref, qb_ref, kb_ref, vb_ref, gb_ref, gn_ref,
                o_ref, sf_ref, sb_ref, osum_ref, dec_ref, tab_ref, car_ref, *, cc, n):
    h = pl.program_id(0) % H_RET
    c = pl.program_id(1)

    @pl.when(c == 0)
    def _():
        sf_ref[...] = jnp.zeros(sf_ref.shape, F32)
        sb_ref[...] = jnp.zeros(sb_ref.shape, F32)

        def log_gamma(d):
            rate = jnp.full((1, 1), rate_ref[d, h], F32)
            return jnp.log1p(-jnp.exp2(rate))

        lgf = log_gamma(0)
        lgb = log_gamma(1)
        ii = lax.broadcasted_iota(I32, (cc, cc), 0).astype(F32)
        jj = lax.broadcasted_iota(I32, (cc, cc), 1).astype(F32)
        dlt = ii - jj
        dec_ref[0] = jnp.where(dlt >= 0, jnp.exp(jnp.where(dlt >= 0, dlt, 0.0) * lgf), 0.0)
        dec_ref[1] = jnp.where(dlt < 0, jnp.exp(jnp.where(dlt < 0, -dlt, 0.0) * lgb), 0.0)
        pos = lax.broadcasted_iota(I32, (cc, HEAD_W), 0).astype(F32)
        tab_ref[0] = jnp.exp((pos + 1.0) * lgf)
        tab_ref[1] = jnp.exp((cc - 1.0 - pos) * lgf)
        tab_ref[2] = jnp.exp((cc - pos) * lgb)
        tab_ref[3] = jnp.exp(pos * lgb)
        car_ref[0] = jnp.broadcast_to(jnp.exp(cc * lgf), car_ref.shape[1:])
        car_ref[1] = jnp.broadcast_to(jnp.exp(cc * lgb), car_ref.shape[1:])

    def direction(q_ref, k_ref, v_ref, s_ref, d):
        q = q_ref[...]
        k = k_ref[...]
        v = v_ref[...]
        s = lax.dot_general(q, k, (((1,), (1,)), ((), ())), preferred_element_type=F32) * dec_ref[d]
        state = s_ref[...]
        o = (jnp.dot(s.astype(BF16), v, preferred_element_type=F32)
             + jnp.dot(q, state.astype(BF16), preferred_element_type=F32) * tab_ref[2 * d])
        kz = (k.astype(F32) * tab_ref[2 * d + 1]).astype(BF16)
        s_ref[...] = car_ref[d, 0:1, 0:1] * state + lax.dot_general(kz, v, (((0,), (0,)), ((), ())),
                                                                    preferred_element_type=F32)
        return o

    o_f = direction(qf_ref, kf_ref, vf_ref, sf_ref, 0)
    o_b = direction(qb_ref, kb_ref, vb_ref, sb_ref, 1)

    rows_f = pl.ds(pl.multiple_of(c * cc, cc), cc)
    rows_b = pl.ds(pl.multiple_of((n - 1 - c) * cc, cc), cc)

    @pl.when(c < n // 2)
    def _():
        osum_ref[rows_f, :] = o_f
        osum_ref[rows_b, :] = o_b

    def finish(o, gate_ref):
        mu = jnp.mean(o, axis=-1, keepdims=True)
        var = jnp.mean(jnp.square(o - mu), axis=-1, keepdims=True)
        y = (o - mu) * lax.rsqrt(var + NORM_EPS) * gn_ref[...]
        gate = gate_ref[...]
        return (y * (gate * _sigmoid(gate))).astype(o_ref.dtype)

    @pl.when(c >= n // 2)
    def _():
        o_ref[rows_f, :] = finish(osum_ref[rows_f, :] + o_f, gf_ref)
        o_ref[rows_b, :] = finish(osum_ref[rows_b, :] + o_b, gb_ref)


def _retention(qk, v, gate, ret_log2_decay, ret_norm_g, seq):
    T = v.shape[0]
    nb = T // seq
    cc = _tile(seq // 2, 256)
    n = seq // cc
    assert n % 2 == 0

    def fwd(col):
        return pl.BlockSpec((cc, HEAD_W), lambda bh, c: ((bh // H_RET) * n + c, (bh % H_RET) + col))

    def bwd(col):
        return pl.BlockSpec((cc, HEAD_W), lambda bh, c: ((bh // H_RET) * n + n - 1 - c, (bh % H_RET) + col))

    return pl.pallas_call(
        functools.partial(_ret_kernel, cc=cc, n=n),
        grid=(nb * H_RET, n),
        in_specs=[pl.BlockSpec(memory_space=pltpu.SMEM),
                  fwd(0), fwd(H_RET), fwd(0), fwd(0), bwd(0), bwd(H_RET), bwd(0), bwd(0),
                  pl.BlockSpec((1, HEAD_W), lambda bh, c: (0, bh % H_RET))],
        out_specs=pl.BlockSpec((seq, HEAD_W), lambda bh, c: (bh // H_RET, bh % H_RET)),
        out_shape=jax.ShapeDtypeStruct((T, GROUP_W), BF16),
        scratch_shapes=[pltpu.VMEM((DK_RET, DV_RET), F32), pltpu.VMEM((DK_RET, DV_RET), F32),
                        pltpu.VMEM((seq, HEAD_W), F32), pltpu.VMEM((2, cc, cc), F32),
                        pltpu.VMEM((4, cc, HEAD_W), F32), pltpu.VMEM((2, 8, LANES), F32)],
        compiler_params=_params(("parallel", "arbitrary")),
        name="retention",
    )(ret_log2_decay, qk, qk, v, gate, qk, qk, v, gate, ret_norm_g.reshape(1, GROUP_W))


def _outproj_kernel(a1_ref, a2_ref, w1_ref, w2_ref, o_ref):
    o_ref[...] = (jnp.dot(a1_ref[...], w1_ref[...], preferred_element_type=F32)
                  + jnp.dot(a2_ref[...], w2_ref[...], preferred_element_type=F32))


def _out_projection(od, orr, wob):
    T = od.shape[0]
    D = wob.shape[1]
    tm = _tile(T, 1024)
    tn = _tile(D, 1024)
    return pl.pallas_call(
        _outproj_kernel,
        grid=(T // tm, D // tn),
        in_specs=[pl.BlockSpec((tm, GROUP_W), lambda i, j: (i, 0)),
                  pl.BlockSpec((tm, GROUP_W), lambda i, j: (i, 0)),
                  pl.BlockSpec((GROUP_W, tn), lambda i, j: (0, j)),
                  pl.BlockSpec((GROUP_W, tn), lambda i, j: (1, j))],
        out_specs=pl.BlockSpec((tm, tn), lambda i, j: (i, j)),
        out_shape=jax.ShapeDtypeStruct((T, D), F32),
        compiler_params=_params(("parallel", "arbitrary")),
        name="out_proj",
    )(od, orr, wob, wob)


def _layer_norm_rows(z, g, b):
    mu = jnp.mean(z, axis=-1, keepdims=True)
    var = jnp.mean(jnp.square(z - mu), axis=-1, keepdims=True)
    return (z - mu) * lax.rsqrt(var + NORM_EPS) * g + b


def _ln1_kernel(*refs, tile_starts):
    ng = len(tile_starts) - 1
    x_refs = refs[:ng]
    mix_ref, g_ref, b_ref, wr_ref, x1_ref, lg_ref = refs[ng:]
    i = pl.program_id(0)
    for gi, x_ref in enumerate(x_refs):
        @pl.when((i >= tile_starts[gi]) & (i < tile_starts[gi + 1]))
        def _(x_ref=x_ref):
            y = _layer_norm_rows(DEEPNORM_ALPHA * x_ref[...] + mix_ref[...], g_ref[...], b_ref[...])
            x1_ref[...] = y
            lg_ref[...] = lax.dot_general(wr_ref[...], y, (((1,), (1,)), ((), ())),
                                          precision=lax.Precision.HIGHEST, preferred_element_type=F32)


def _ln1_router(x_groups2d, mix, g, b, w_router_t):
    T, D = mix.shape
    tr = _tile(math.gcd(*[x.shape[0] for x in x_groups2d]), 256)
    tile_starts = [0]
    for x in x_groups2d:
        tile_starts.append(tile_starts[-1] + x.shape[0] // tr)

    def x_spec(gi):
        lo, hi = tile_starts[gi], tile_starts[gi + 1]
        return pl.BlockSpec((tr, D), lambda i: (jnp.clip(i - lo, 0, hi - lo - 1), 0))

    return pl.pallas_call(
        functools.partial(_ln1_kernel, tile_starts=tuple(tile_starts)),
        grid=(T // tr,),
        in_specs=[x_spec(gi) for gi in range(len(x_groups2d))] + [
                  pl.BlockSpec((tr, D), lambda i: (i, 0)),
                  pl.BlockSpec((1, D), lambda i: (0, 0)),
                  pl.BlockSpec((1, D), lambda i: (0, 0)),
                  pl.BlockSpec((N_EXPERTS, D), lambda i: (0, 0))],
        out_specs=[pl.BlockSpec((tr, D), lambda i: (i, 0)),
                   pl.BlockSpec((N_EXPERTS, tr), lambda i: (0, i))],
        out_shape=[jax.ShapeDtypeStruct((T, D), F32), jax.ShapeDtypeStruct((N_EXPERTS, T), F32)],
        compiler_params=_params(("parallel",)),
        name="ln1_router",
    )(*x_groups2d, mix, g.reshape(1, D), b.reshape(1, D), w_router_t)


def _route_kernel(lg_ref, idx_ref, pk_ref, pos_ref, gk_ref, mk_ref, kmax_ref, *, cap, tok0):
    E, R, _ = lg_ref.shape
    logits = lg_ref[...]
    ex = jnp.exp(logits - jnp.max(logits, axis=0, keepdims=True))
    aff = ex / jnp.sum(ex, axis=0, keepdims=True)
    bits = pltpu.bitcast(aff, I32)

    def count(pred):
        c = jnp.sum(jnp.where(pred, 1.0, 0.0), axis=2, keepdims=True)
        return jnp.sum(c, axis=1, keepdims=True)

    capf = jnp.float32(cap)

    def thr_step(i, thr):
        cand = thr | lax.shift_left(jnp.int32(1), 30 - i)
        return jnp.where(count(bits >= cand) >= capf, cand, thr)

    thr = lax.fori_loop(0, 31, thr_step, jnp.zeros((E, 1, 1), I32))
    gt = bits > thr
    eq = bits == thr
    need = capf - count(gt)
    tid = (lax.broadcasted_iota(I32, (E, R, LANES), 1) * LANES
           + lax.broadcasted_iota(I32, (E, R, LANES), 2))
    nbits = int(math.log2(R * LANES))

    def tie_step(i, v):
        cand = v | lax.shift_left(jnp.int32(1), nbits - 1 - i)
        return jnp.where(count(eq & (tid < cand)) < need, cand, v)

    vtie = lax.fori_loop(0, nbits, tie_step, jnp.zeros((E, 1, 1), I32))
    mask = jnp.where(gt | (eq & (tid <= vtie)), 1.0, 0.0)

    ranks = []
    cnt = jnp.zeros((R, LANES), F32)
    for e in range(E):
        ranks.append(cnt)
        cnt = cnt + mask[e]
    for k in range(E):
        gk = jnp.zeros((R, LANES), F32)
        for e in range(k, E):
            gk = gk + jnp.where((ranks[e] == k) & (mask[e] > 0), aff[e], 0.0)
        gk_ref[k] = gk
        mk_ref[k] = jnp.where(cnt > k, 1.0, 0.0)
    kmax_ref[...] = jnp.zeros(kmax_ref.shape, I32)
    for t in range(LANES // COMBINE_TOKENS):
        tile_cnt = cnt[:, t * COMBINE_TOKENS:(t + 1) * COMBINE_TOKENS]
        kmax_ref[:, t:t + 1] = jnp.max(tile_cnt, axis=1, keepdims=True).astype(I32)
    lane_col = lax.broadcasted_iota(I32, (LANES, 1), 0).astype(F32)

    mb = mask.astype(BF16)
    li = lax.broadcasted_iota(I32, (LANES, LANES), 0)
    lj = lax.broadcasted_iota(I32, (LANES, LANES), 1)
    upper = jnp.where(li <= lj, 1.0, 0.0).astype(BF16)
    lower = jnp.where(li >= lj, 1.0, 0.0).astype(BF16)
    ri = lax.broadcasted_iota(I32, (R, R), 0)
    rj = lax.broadcasted_iota(I32, (R, R), 1)
    strict = jnp.where(rj < ri, 1.0, 0.0).astype(BF16)
    slot = lax.broadcasted_iota(I32, (1, cap), 1).astype(F32)
    rcol = lax.broadcasted_iota(I32, (R, 1), 0).astype(F32)
    for e in range(E):
        me = mb[e]
        winc = jnp.dot(me, upper, preferred_element_type=F32)
        rows = winc[:, LANES - 1:LANES]
        offs = jnp.sum(jnp.dot(strict, me, preferred_element_type=F32), axis=-1, keepdims=True)
        pos_ref[e] = (offs + winc - mask[e]).astype(I32)
        ends = offs + rows
        rs = jnp.sum(jnp.where(ends <= slot, 1.0, 0.0), axis=0, keepdims=True)
        onehot = rcol == rs
        offs_s = jnp.sum(jnp.where(onehot, offs, 0.0), axis=0, keepdims=True)
        winc_t = lax.dot_general(lower, me, (((1,), (1,)), ((), ())), preferred_element_type=F32)
        wg = jnp.dot(winc_t.astype(BF16), jnp.where(onehot, 1.0, 0.0).astype(BF16),
                     preferred_element_type=F32)
        ls = jnp.sum(jnp.where(wg <= slot - offs_s, 1.0, 0.0), axis=0, keepdims=True)
        tok = (rs * LANES + ls).astype(I32)
        idx_ref[e:e + 1, :] = tok + tok0
        rg = lax.dot_general(ranks[e].astype(BF16), jnp.where(onehot, 1.0, 0.0).astype(BF16),
                             (((0,), (0,)), ((), ())), preferred_element_type=F32)
        rk = jnp.sum(jnp.where(lane_col == ls, rg, 0.0), axis=0, keepdims=True).astype(I32)
        pk_ref[e:e + 1, :] = tok * E + rk


def _route(logits_t, cap, tok0):
    E, Tg = logits_t.shape
    R = Tg // LANES
    per_tok = jax.ShapeDtypeStruct((E, R, LANES), F32)
    idx, pk, pos, gk, mk, kmax = pl.pallas_call(
        functools.partial(_route_kernel, cap=cap, tok0=tok0),
        out_shape=[jax.ShapeDtypeStruct((E, cap), I32), jax.ShapeDtypeStruct((E, cap), I32),
                   jax.ShapeDtypeStruct((E, R, LANES), I32), per_tok, per_tok,
                   jax.ShapeDtypeStruct((R, LANES), I32)],
        compiler_params=pltpu.CompilerParams(vmem_limit_bytes=V7X_VMEM_LIMIT_BYTES),
        name="route",
    )(logits_t.reshape(E, R, LANES))
    kmax = kmax[:, :LANES // COMBINE_TOKENS].reshape(-1)
    return idx, pk, pos.reshape(E, Tg), gk.reshape(E, Tg), mk.reshape(E, Tg), kmax


def _gather_kernel(idx_ref, idx_next_ref, x_hbm, o_ref, buf, sem):
    i = pl.program_id(0)
    tg = buf.shape[1]

    def row_copy(src_row, slot, dst_row):
        return pltpu.make_async_copy(x_hbm.at[pl.ds(src_row, 1)], buf.at[slot, pl.ds(dst_row, 1)], sem.at[slot])

    def issue(ref, slot):
        def body(r, carry):
            row_copy(ref[0, 0, r], slot, r).start()
            return carry
        lax.fori_loop(0, tg, body, 0, unroll=8)

    @pl.when(i == 0)
    def _():
        issue(idx_ref, 0)

    @pl.when(i + 1 < pl.num_programs(0))
    def _():
        issue(idx_next_ref, (i + 1) % 2)

    slot = i % 2
    for _ in range(tg):
        row_copy(0, slot, 0).wait()
    o_ref[...] = buf[slot].astype(o_ref.dtype)


def _gather_tokens(x1, idx_all):
    T, D = x1.shape
    n_rows = idx_all.size
    tg = _tile(n_rows, 256)
    n_tiles = n_rows // tg
    idx_tiles = idx_all.reshape(n_tiles, 1, tg)
    return pl.pallas_call(
        _gather_kernel,
        grid=(n_tiles,),
        in_specs=[pl.BlockSpec((1, 1, tg), lambda i: (i, 0, 0), memory_space=pltpu.SMEM),
                  pl.BlockSpec((1, 1, tg), lambda i: (jnp.minimum(i + 1, n_tiles - 1), 0, 0),
                               memory_space=pltpu.SMEM),
                  pl.BlockSpec(memory_space=pl.ANY)],
        out_specs=pl.BlockSpec((tg, D), lambda i: (i, 0)),
        out_shape=jax.ShapeDtypeStruct((n_rows, D), BF16),
        scratch_shapes=[pltpu.VMEM((2, tg, D), F32), pltpu.SemaphoreType.DMA((2,))],
        compiler_params=_params(("arbitrary",)),
        name="gather_tokens",
    )(idx_tiles, idx_tiles, x1)


def _ffn_a_kernel(x_ref, wg_ref, wu_ref, h_ref):
    x = x_ref[...]
    g = jnp.dot(x, wg_ref[...].astype(BF16), preferred_element_type=F32)
    u = jnp.dot(x, wu_ref[...].astype(BF16), preferred_element_type=F32)
    h_ref[...] = (g * _sigmoid(g) * u).astype(h_ref.dtype)


def _ffn_hidden(xe, w_gate, w_up):
    E, Ctot, D = xe.shape
    F = w_gate.shape[2]
    tm = _tile(Ctot, 1536)
    tf = _tile(F, 256)
    return pl.pallas_call(
        _ffn_a_kernel,
        grid=(E, Ctot // tm, F // tf),
        in_specs=[pl.BlockSpec((None, tm, D), lambda e, m, f: (e, m, 0)),
                  pl.BlockSpec((None, D, tf), lambda e, m, f: (e, 0, f)),
                  pl.BlockSpec((None, D, tf), lambda e, m, f: (e, 0, f))],
        out_specs=pl.BlockSpec((None, tm, tf), lambda e, m, f: (e, m, f)),
        out_shape=jax.ShapeDtypeStruct((E, Ctot, F), BF16),
        compiler_params=_params(("parallel", "parallel", "arbitrary")),
        name="ffn_hidden",
    )(xe, w_gate, w_up)


def _ffn_b_kernel(h_ref, wd_ref, y_ref):
    @pl.when(pl.program_id(3) == 0)
    def _():
        y_ref[...] = jnp.zeros(y_ref.shape, F32)

    y_ref[...] += jnp.dot(h_ref[...], wd_ref[...].astype(BF16), preferred_element_type=F32)


def _ffn_out(h, w_down):
    E, Ctot, F = h.shape
    D = w_down.shape[2]
    tm = _tile(Ctot, 1536)
    tn = _tile(D, 1024)
    tk = _tile(F, 2048)
    return pl.pallas_call(
        _ffn_b_kernel,
        grid=(E, Ctot // tm, D // tn, F // tk),
        in_specs=[pl.BlockSpec((None, tm, tk), lambda e, m, n, k: (e, m, k)),
                  pl.BlockSpec((None, tk, tn), lambda e, m, n, k: (e, k, n))],
        out_specs=pl.BlockSpec((None, tm, tn), lambda e, m, n, k: (e, m, n)),
        out_shape=jax.ShapeDtypeStruct((E, Ctot, D), F32),
        compiler_params=_params(("parallel", "parallel", "parallel", "arbitrary")),
        name="ffn_out",
    )(h, w_down)


def _combine_kernel(starts_ref, pk_ref, kmax_ref, ye_hbm, gk_ref, mk_ref, x1_ref, g_ref, b_ref, o_ref, buf, sem,
                    *, row0, ctot):
    E = buf.shape[1]
    tt = x1_ref.shape[0]
    j = pl.program_id(0)

    def row_copy(src_row, slot, k, dst_row):
        return pltpu.make_async_copy(ye_hbm.at[pl.ds(src_row, 1)], buf.at[slot, k, pl.ds(dst_row, 1)],
                                     sem.at[slot])

    def issue(tile, slot):
        for e in range(E):
            s0 = starts_ref[e, tile]
            n = starts_ref[e, tile + 1] - s0

            def one(s, e=e):
                word = pk_ref[e, s]
                tok = lax.shift_right_logical(word, int(math.log2(E)))
                row_copy(row0 + e * ctot + s, slot, word & (E - 1), tok - tile * tt).start()

            def body4(i, carry, s0=s0, one=one):
                for u in range(4):
                    one(s0 + 4 * i + u)
                return carry

            def body1(i, carry, s0=s0, n=n, one=one):
                one(s0 + (n & ~3) + i)
                return carry

            lax.fori_loop(0, lax.shift_right_logical(n, 2), body4, 0)
            lax.fori_loop(0, n & 3, body1, 0)

    @pl.when(j == 0)
    def _():
        buf[...] = jnp.zeros(buf.shape, F32)
        issue(0, 0)

    @pl.when(j + 1 < pl.num_programs(0))
    def _():
        issue(j + 1, (j + 1) % 2)

    slot = j % 2
    n_rows = jnp.int32(0)
    for e in range(E):
        n_rows = n_rows + (starts_ref[e, j + 1] - starts_ref[e, j])

    def drain8(r, carry):
        for _ in range(8):
            row_copy(0, slot, 0, 0).wait()
        return carry

    def drain1(r, carry):
        row_copy(0, slot, 0, 0).wait()
        return carry

    lax.fori_loop(0, lax.shift_right_logical(n_rows, 3), drain8, 0)
    lax.fori_loop(0, n_rows & 7, drain1, 0)

    o_ref[...] = DEEPNORM_ALPHA * x1_ref[...]
    gk = gk_ref[...]
    mk = mk_ref[...]
    n_k = kmax_ref[j]

    def picked(k):
        return jnp.where(mk[:, k:k + 1] > 0, buf[slot, k] * gk[:, k:k + 1], 0.0)

    for k in range(0, E, 2):
        @pl.when(k < n_k)
        def _(k=k):
            o_ref[...] += picked(k) + picked(k + 1)
    o_ref[...] = _layer_norm_rows(o_ref[...], g_ref[...], b_ref[...])


def _combine(ye_rows, starts, pk, kmax, gk_t, mk_t, x1, g, b, tok0, row0, ctot):
    Tg = gk_t.shape[0]
    D = x1.shape[1]
    tt = COMBINE_TOKENS
    blk0 = tok0 // tt
    grid_spec = pltpu.PrefetchScalarGridSpec(
        num_scalar_prefetch=3,
        grid=(Tg // tt,),
        in_specs=[pl.BlockSpec(memory_space=pl.ANY),
                  pl.BlockSpec((tt, N_EXPERTS), lambda j, *_: (j, 0)),
                  pl.BlockSpec((tt, N_EXPERTS), lambda j, *_: (j, 0)),
                  pl.BlockSpec((tt, D), lambda j, *_: (j + blk0, 0)),
                  pl.BlockSpec((1, D), lambda j, *_: (0, 0)),
                  pl.BlockSpec((1, D), lambda j, *_: (0, 0))],
        out_specs=pl.BlockSpec((tt, D), lambda j, *_: (j, 0)),
        scratch_shapes=[pltpu.VMEM((2, N_EXPERTS, tt, D), F32), pltpu.SemaphoreType.DMA((2,))],
    )
    return pl.pallas_call(
        functools.partial(_combine_kernel, row0=row0, ctot=ctot),
        grid_spec=grid_spec,
        out_shape=jax.ShapeDtypeStruct((Tg, D), F32),
        compiler_params=_params(("arbitrary",)),
        name="combine_ln2",
    )(starts, pk, kmax, ye_rows, gk_t, mk_t, x1, g.reshape(1, D), b.reshape(1, D))


def _encoder_layer(x_groups, layer_idx, w_in, w_out, diff_lambda, diff_norm_g, ret_log2_decay, ret_norm_g,
                   ln_g, ln_b, w_router, w_gate, w_up, w_down):
    seq = x_groups[0].shape[1]
    D = x_groups[0].shape[2]
    assert all(x.shape[1] == seq for x in x_groups)
    lam_init = 0.8 - 0.6 * math.exp(-0.3 * layer_idx)
    x2d = [g.reshape(-1, D) for g in x_groups]
    xb = jnp.concatenate([g.astype(BF16) for g in x2d], axis=0)
    wb = w_in.astype(BF16)
    wob = w_out.astype(BF16)

    c128, s128 = _rope_tables(seq, D_DIFF)
    c256, s256 = _rope_tables(seq, DK_RET)
    tab128 = (jnp.concatenate([c128, c128], axis=1), jnp.concatenate([-s128, s128], axis=1))
    tab256 = (c256, s256)

    qkd = _project(xb, wb, 0 * GROUP_W, 2 * GROUP_W, "rot128", BF16, seq, tab128)
    vd = _project(xb, wb, 2 * GROUP_W, GROUP_W, "plain", BF16, seq)
    qkr = _project(xb, wb, 3 * GROUP_W, 2 * GROUP_W, "rot256", BF16, seq, tab256)
    vr = _project(xb, wb, 5 * GROUP_W, GROUP_W, "plain", BF16, seq)
    gr = _project(xb, wb, 6 * GROUP_W, GROUP_W, "plain", F32, seq)

    od = _diff_attention(qkd, vd, diff_lambda, diff_norm_g, seq, lam_init)
    orr = _retention(qkr, vr, gr, ret_log2_decay, ret_norm_g, seq)
    mix = _out_projection(od, orr, wob)
    x1, logits_t = _ln1_router(x2d, mix, ln_g[0], ln_b[0], w_router.T)

    group_tokens = [g.shape[0] * seq for g in x_groups]
    caps = [min(t, max(1, EC_CAPACITY * t // N_EXPERTS)) for t in group_tokens]
    ctot = sum(caps)
    routed = []
    tok0 = 0
    for tg, cap in zip(group_tokens, caps):
        routed.append(_route(logits_t[:, tok0:tok0 + tg], cap, tok0))
        tok0 += tg
    idx_all = jnp.concatenate([r[0] for r in routed], axis=1)
    xe = _gather_tokens(x1, idx_all).reshape(N_EXPERTS, ctot, D)
    h = _ffn_hidden(xe, w_gate, w_up)
    ye = _ffn_out(h, w_down).reshape(N_EXPERTS * ctot, D)

    outs = []
    tok0 = 0
    row0 = 0
    for g, tg, cap, (_, pk, pos, gk, mk, kmax) in zip(x_groups, group_tokens, caps, routed):
        starts = jnp.concatenate([pos[:, ::COMBINE_TOKENS], jnp.full((N_EXPERTS, 1), cap, I32)], axis=1)
        y = _combine(ye, starts, pk, kmax, gk.T, mk.T, x1, ln_g[1], ln_b[1], tok0, row0, ctot)
        outs.append(y.reshape(g.shape))
        tok0 += tg
        row0 += cap
    return outs


def kernel(x_prompt, x_sample, w_in, w_out, diff_lambda, diff_norm_g, ret_log2_decay, ret_norm_g, ln_g, ln_b,
           w_router, w_gate, w_up, w_down):
    xs = [x_prompt, x_sample]
    for l in range(w_in.shape[0]):
        xs = _encoder_layer(xs, l, w_in[l], w_out[l], diff_lambda[l], diff_norm_g[l], ret_log2_decay[l],
                            ret_norm_g[l], ln_g[l], ln_b[l], w_router[l], w_gate[l], w_up[l], w_down[l])
    return (xs[0], xs[1])
```

```python
import functools
import math

import jax
import jax.numpy as jnp
from jax import lax
from jax.experimental import pallas as pl
from jax.experimental.pallas import tpu as pltpu

F32 = jnp.float32
BF16 = jnp.bfloat16
I32 = jnp.int32

H_DIFF = 8
D_DIFF = 128
H_RET = 8
DK_RET = 256
DV_RET = 256
HEAD_W = 256
GROUP_W = H_DIFF * HEAD_W
N_EXPERTS = 16
EC_CAPACITY = 2
ROPE_THETA = 10000.0
NORM_EPS = 1e-5
DEPTH = 1
DEEPNORM_ALPHA = (2.0 * DEPTH) ** 0.25
ATTN_Q_SCALE = (D_DIFF ** -0.5) * math.log2(math.e)

LANES = 128
V7X_VMEM_LIMIT_BYTES = 56 * 2**20
COMBINE_TOKENS = 64


def _tile(dim, pref):
    t = min(dim, pref)
    while dim % t:
        t //= 2
    return t


def _params(sem):
    return pltpu.CompilerParams(dimension_semantics=sem, vmem_limit_bytes=V7X_VMEM_LIMIT_BYTES)


def _sigmoid(x):
    return 1.0 / (1.0 + jnp.exp(-x))


def _proj_kernel(x_ref, w_ref, *rest, kind, n_q_blocks):
    if kind == "plain":
        (o_ref,) = rest
    else:
        cos_ref, sin_ref, o_ref = rest
    acc = jnp.dot(x_ref[...], w_ref[...], preferred_element_type=F32)
    tn = acc.shape[1]
    if kind == "plain":
        o_ref[...] = acc.astype(o_ref.dtype)
    elif kind == "rot128":
        scale = jnp.where(pl.program_id(1) < n_q_blocks, ATTN_Q_SCALE, 1.0).astype(F32)
        cos = cos_ref[...] * scale
        sin = sin_ref[...] * scale
        for c in range(tn // LANES):
            xc = acc[:, c * LANES:(c + 1) * LANES]
            o_ref[:, c * LANES:(c + 1) * LANES] = (xc * cos + pltpu.roll(xc, 64, 1) * sin).astype(o_ref.dtype)
    else:
        cos = cos_ref[...]
        sin = sin_ref[...]
        scale = jnp.where(pl.program_id(1) >= n_q_blocks, DK_RET ** -0.5, 1.0).astype(F32)
        for c in range(tn // HEAD_W):
            x1 = acc[:, c * HEAD_W:c * HEAD_W + LANES]
            x2 = acc[:, c * HEAD_W + LANES:(c + 1) * HEAD_W]
            o_ref[:, c * HEAD_W:c * HEAD_W + LANES] = ((x1 * cos - x2 * sin) * scale).astype(o_ref.dtype)
            o_ref[:, c * HEAD_W + LANES:(c + 1) * HEAD_W] = ((x2 * cos + x1 * sin) * scale).astype(o_ref.dtype)


def _project(xb, wb, col0, ncols, kind, out_dtype, seq, tables=None):
    T, D = xb.shape
    tm = _tile(seq, 1024)
    tn = _tile(ncols, 1024)
    nb0 = col0 // tn
    in_specs = [pl.BlockSpec((tm, D), lambda i, j: (i, 0)),
                pl.BlockSpec((D, tn), lambda i, j: (0, j + nb0))]
    args = [xb, wb]
    if kind != "plain":
        npos = seq // tm
        in_specs += [pl.BlockSpec((tm, LANES), lambda i, j: (i % npos, 0))] * 2
        args += list(tables)
    return pl.pallas_call(
        functools.partial(_proj_kernel, kind=kind, n_q_blocks=GROUP_W // tn),
        grid=(T // tm, ncols // tn),
        in_specs=in_specs,
        out_specs=pl.BlockSpec((tm, tn), lambda i, j: (i, j)),
        out_shape=jax.ShapeDtypeStruct((T, ncols), out_dtype),
        compiler_params=_params(("parallel", "arbitrary")),
        name="in_proj_" + kind,
    )(*args)


def _rope_tables(seq, d):
    inv_freq = 1.0 / (ROPE_THETA ** (jnp.arange(0, d, 2, dtype=F32) / d))
    ang = jnp.arange(seq, dtype=F32)[:, None] * inv_freq[None, :]
    return jnp.cos(ang), jnp.sin(ang)


ATTN_ROW_CHUNK = 16


def _attn_kernel(q_ref, k_ref, v_ref, lam_ref, g_ref, o_ref, m_ref, l_ref, mp_ref, acc_ref, s_ref, p_ref,
                 *, tk, lam_init):
    seq = k_ref.shape[0]
    tq = q_ref.shape[0]
    rc = min(ATTN_ROW_CHUNK, tq)
    m_ref[...] = jnp.full(m_ref.shape, -jnp.inf, F32)
    l_ref[...] = jnp.zeros(l_ref.shape, F32)
    acc_ref[...] = jnp.zeros(acc_ref.shape, F32)

    nt = tk // LANES

    def lane_fold(x, op):
        y = x[:, :LANES]
        for t in range(1, nt):
            y = op(y, x[:, t * LANES:(t + 1) * LANES])
        return y

    def body(i, carry):
        off = pl.multiple_of(i * tk, tk)
        for c in range(2):
            s_ref[c] = lax.dot_general(q_ref[:, c * D_DIFF:(c + 1) * D_DIFF],
                                       k_ref[pl.ds(off, tk), c * D_DIFF:(c + 1) * D_DIFF],
                                       (((1,), (1,)), ((), ())), preferred_element_type=F32)
        for c in range(2):
            for r in range(tq // rc):
                rows = slice(r * rc, (r + 1) * rc)
                mp_ref[rows, :] = lane_fold(s_ref[c, rows, :], jnp.maximum)
            m_old = m_ref[c]
            m_new = jnp.maximum(m_old, jnp.max(mp_ref[...], axis=-1, keepdims=True))
            alpha = jnp.exp2(m_old - m_new)
            m_ref[c] = m_new
            for r in range(tq // rc):
                rows = slice(r * rc, (r + 1) * rc)
                mrow = m_ref[c, rows, :]
                p = jnp.exp2(s_ref[c, rows, :] - jnp.concatenate([mrow] * nt, axis=1))
                mp_ref[rows, :] = lane_fold(p, jnp.add)
                p_ref[c, rows, :] = p.astype(BF16)
            l_ref[c] = alpha * l_ref[c] + mp_ref[...]
            acc_ref[c] = (jnp.concatenate([alpha] * (HEAD_W // LANES), axis=1) * acc_ref[c]
                          + jnp.dot(p_ref[c], v_ref[pl.ds(off, tk), :], preferred_element_type=F32))
        return carry

    lax.fori_loop(0, seq // tk, body, 0)

    dl = lam_ref[...]
    lam = (jnp.exp(jnp.sum(dl[0:1] * dl[1:2], axis=-1, keepdims=True))
           - jnp.exp(jnp.sum(dl[2:3] * dl[3:4], axis=-1, keepdims=True)) + lam_init)
    l0 = jnp.sum(l_ref[0], axis=-1, keepdims=True)
    l1 = jnp.sum(l_ref[1], axis=-1, keepdims=True)
    o = acc_ref[0] / l0 - lam * (acc_ref[1] / l1)
    o = o * lax.rsqrt(jnp.mean(o * o, axis=-1, keepdims=True) + NORM_EPS)
    o_ref[...] = (o * g_ref[...] * (1.0 - lam_init)).astype(o_ref.dtype)


def _diff_attention(qk, v, diff_lambda, diff_norm_g, seq, lam_init):
    T = v.shape[0]
    nb = T // seq
    tq = _tile(seq, 1024)
    tk = _tile(seq, 1024)
    nq = seq // tq
    return pl.pallas_call(
        functools.partial(_attn_kernel, tk=tk, lam_init=lam_init),
        grid=(nb, H_DIFF, nq),
        in_specs=[pl.BlockSpec((tq, HEAD_W), lambda b, h, i: (b * nq + i, h)),
                  pl.BlockSpec((seq, HEAD_W), lambda b, h, i: (b, H_DIFF + h)),
                  pl.BlockSpec((seq, HEAD_W), lambda b, h, i: (b, h)),
                  pl.BlockSpec((4, D_DIFF), lambda b, h, i: (0, 0)),
                  pl.BlockSpec((1, HEAD_W), lambda b, h, i: (0, 0))],
        out_specs=pl.BlockSpec((tq, HEAD_W), lambda b, h, i: (b * nq + i, h)),
        out_shape=jax.ShapeDtypeStruct((T, GROUP_W), BF16),
        scratch_shapes=[pltpu.VMEM((2, tq, LANES), F32), pltpu.VMEM((2, tq, LANES), F32),
                        pltpu.VMEM((tq, LANES), F32),
                        pltpu.VMEM((2, tq, HEAD_W), F32), pltpu.VMEM((2, tq, tk), F32),
                        pltpu.VMEM((2, tq, tk), BF16)],
        compiler_params=_params(("parallel", "parallel", "arbitrary")),
        name="diff_attention",
    )(qk, qk, v, diff_lambda, diff_norm_g.reshape(1, HEAD_W))


RET_HEADS_PER_STEP = 2


def _ret_kernel(rate_ref, qf_ref, kf_ref, vf_ref, gf_ref, qb_ref, kb_ref, vb_ref, gb_ref, gn_ref,
                o_ref, st_ref, osum_ref, dec_ref, tab_ref, car_ref, *, cc, n):
    hps = RET_HEADS_PER_STEP
    h0 = (pl.program_id(0) % (H_RET // hps)) * hps
    c = pl.program_id(1)

    @pl.when(c == 0)
    def _():
        st_ref[...] = jnp.zeros(st_ref.shape, F32)
        ii = lax.broadcasted_iota(I32, (cc, cc), 0).astype(F32)
        jj = lax.broadcasted_iota(I32, (cc, cc), 1).astype(F32)
        dlt = ii - jj
        pos = lax.broadcasted_iota(I32, (cc, HEAD_W), 0).astype(F32)
        for hh in range(hps):
            def log_gamma(d, hh=hh):
                rate = jnp.full((1, 1), rate_ref[d, h0 + hh], F32)
                return jnp.log1p(-jnp.exp2(rate))

            lgf = log_gamma(0)
            lgb = log_gamma(1)
            dec_ref[hh, 0] = jnp.where(dlt >= 0, jnp.exp(jnp.where(dlt >= 0, dlt, 0.0) * lgf), 0.0)
            dec_ref[hh, 1] = jnp.where(dlt < 0, jnp.exp(jnp.where(dlt < 0, -dlt, 0.0) * lgb), 0.0)
            tab_ref[hh, 0] = jnp.exp((pos + 1.0) * lgf)
            tab_ref[hh, 1] = jnp.exp((cc - 1.0 - pos) * lgf)
            tab_ref[hh, 2] = jnp.exp((cc - pos) * lgb)
            tab_ref[hh, 3] = jnp.exp(pos * lgb)
            car_ref[hh, 0] = jnp.broadcast_to(jnp.exp(cc * lgf), car_ref.shape[2:])
            car_ref[hh, 1] = jnp.broadcast_to(jnp.exp(cc * lgb), car_ref.shape[2:])

    def direction(q_ref, k_ref, v_ref, hh, d):
        cols = slice(hh * HEAD_W, (hh + 1) * HEAD_W)
        q = q_ref[:, cols]
        k = k_ref[:, cols]
        v = v_ref[:, cols]
        s = lax.dot_general(q, k, (((1,), (1,)), ((), ())), preferred_element_type=F32) * dec_ref[hh, d]
        state = st_ref[hh, d]
        o = (jnp.dot(s.astype(BF16), v, preferred_element_type=F32)
             + jnp.dot(q, state.astype(BF16), preferred_element_type=F32) * tab_ref[hh, 2 * d])
        kz = (k.astype(F32) * tab_ref[hh, 2 * d + 1]).astype(BF16)
        st_ref[hh, d] = car_ref[hh, d, 0:1, 0:1] * state + lax.dot_general(
            kz, v, (((0,), (0,)), ((), ())), preferred_element_type=F32)
        return o

    rows_f = pl.ds(pl.multiple_of(c * cc, cc), cc)
    rows_b = pl.ds(pl.multiple_of((n - 1 - c) * cc, cc), cc)

    def finish(o, gate_ref, cols):
        mu = jnp.mean(o, axis=-1, keepdims=True)
        var = jnp.mean(jnp.square(o - mu), axis=-1, keepdims=True)
        y = (o - mu) * lax.rsqrt(var + NORM_EPS) * gn_ref[:, cols]
        gate = gate_ref[:, cols]
        return (y * (gate * _sigmoid(gate))).astype(o_ref.dtype)

    outs = [(direction(qf_ref, kf_ref, vf_ref, hh, 0), direction(qb_ref, kb_ref, vb_ref, hh, 1))
            for hh in range(hps)]

    @pl.when(c < n // 2)
    def _():
        for hh, (o_f, o_b) in enumerate(outs):
            cols = slice(hh * HEAD_W, (hh + 1) * HEAD_W)
            osum_ref[rows_f, cols] = o_f
            osum_ref[rows_b, cols] = o_b

    @pl.when(c >= n // 2)
    def _():
        for hh, (o_f, o_b) in enumerate(outs):
            cols = slice(hh * HEAD_W, (hh + 1) * HEAD_W)
            o_ref[rows_f, cols] = finish(osum_ref[rows_f, cols] + o_f, gf_ref, cols)
            o_ref[rows_b, cols] = finish(osum_ref[rows_b, cols] + o_b, gb_ref, cols)


def _retention(qk, v, gate, ret_log2_decay, ret_norm_g, seq):
    T = v.shape[0]
    nb = T // seq
    cc = _tile(seq // 2, 256)
    n = seq // cc
    assert n % 2 == 0

    hps = RET_HEADS_PER_STEP
    hg = H_RET // hps
    bw = hps * HEAD_W

    def fwd(col):
        return pl.BlockSpec((cc, bw), lambda bh, c: ((bh // hg) * n + c, (bh % hg) + col))

    def bwd(col):
        return pl.BlockSpec((cc, bw), lambda bh, c: ((bh // hg) * n + n - 1 - c, (bh % hg) + col))

    return pl.pallas_call(
        functools.partial(_ret_kernel, cc=cc, n=n),
        grid=(nb * hg, n),
        in_specs=[pl.BlockSpec(memory_space=pltpu.SMEM),
                  fwd(0), fwd(hg), fwd(0), fwd(0), bwd(0), bwd(hg), bwd(0), bwd(0),
                  pl.BlockSpec((1, bw), lambda bh, c: (0, bh % hg))],
        out_specs=pl.BlockSpec((seq, bw), lambda bh, c: (bh // hg, bh % hg)),
        out_shape=jax.ShapeDtypeStruct((T, GROUP_W), BF16),
        scratch_shapes=[pltpu.VMEM((hps, 2, DK_RET, DV_RET), F32),
                        pltpu.VMEM((seq, bw), F32), pltpu.VMEM((hps, 2, cc, cc), F32),
                        pltpu.VMEM((hps, 4, cc, HEAD_W), F32), pltpu.VMEM((hps, 2, 8, LANES), F32)],
        compiler_params=_params(("parallel", "arbitrary")),
        name="retention",
    )(ret_log2_decay, qk, qk, v, gate, qk, qk, v, gate, ret_norm_g.reshape(1, GROUP_W))


def _outproj_kernel(*refs, tile_starts):
    ng = len(tile_starts) - 1
    x_refs = refs[:ng]
    a1_ref, a2_ref, w1_ref, w2_ref, o_ref = refs[ng:]
    mix = (jnp.dot(a1_ref[...], w1_ref[...], preferred_element_type=F32)
           + jnp.dot(a2_ref[...], w2_ref[...], preferred_element_type=F32))
    i = pl.program_id(0)
    for gi, x_ref in enumerate(x_refs):
        @pl.when((i >= tile_starts[gi]) & (i < tile_starts[gi + 1]))
        def _(x_ref=x_ref):
            o_ref[...] = DEEPNORM_ALPHA * x_ref[...] + mix


def _out_projection(x_groups2d, od, orr, wob):
    T = od.shape[0]
    D = wob.shape[1]
    tm = _tile(math.gcd(*[x.shape[0] for x in x_groups2d]), 1024)
    tn = _tile(D, 512)
    tile_starts = [0]
    for x in x_groups2d:
        tile_starts.append(tile_starts[-1] + x.shape[0] // tm)

    def x_spec(gi):
        lo, hi = tile_starts[gi], tile_starts[gi + 1]
        return pl.BlockSpec((tm, tn), lambda i, j: (jnp.clip(i - lo, 0, hi - lo - 1), j))

    return pl.pallas_call(
        functools.partial(_outproj_kernel, tile_starts=tuple(tile_starts)),
        grid=(T // tm, D // tn),
        in_specs=[x_spec(gi) for gi in range(len(x_groups2d))] + [
                  pl.BlockSpec((tm, GROUP_W), lambda i, j: (i, 0)),
                  pl.BlockSpec((tm, GROUP_W), lambda i, j: (i, 0)),
                  pl.BlockSpec((GROUP_W, tn), lambda i, j: (0, j)),
                  pl.BlockSpec((GROUP_W, tn), lambda i, j: (1, j))],
        out_specs=pl.BlockSpec((tm, tn), lambda i, j: (i, j)),
        out_shape=jax.ShapeDtypeStruct((T, D), F32),
        compiler_params=_params(("parallel", "arbitrary")),
        name="out_proj",
    )(*x_groups2d, od, orr, wob, wob)


def _layer_norm_rows(z, g, b):
    mu = jnp.mean(z, axis=-1, keepdims=True)
    var = jnp.mean(jnp.square(z - mu), axis=-1, keepdims=True)
    return (z - mu) * lax.rsqrt(var + NORM_EPS) * g + b


def _ln1_kernel(z_ref, g_ref, b_ref, wr_ref, x1_ref, lg_ref):
    y = _layer_norm_rows(z_ref[...], g_ref[...], b_ref[...])
    x1_ref[...] = y
    lg_ref[...] = lax.dot_general(wr_ref[...], y, (((1,), (1,)), ((), ())),
                                  precision=lax.Precision.HIGHEST, preferred_element_type=F32)


def _ln1_router(z, g, b, w_router_t):
    T, D = z.shape
    tr = _tile(T, 256)
    return pl.pallas_call(
        _ln1_kernel,
        grid=(T // tr,),
        in_specs=[pl.BlockSpec((tr, D), lambda i: (i, 0)),
                  pl.BlockSpec((1, D), lambda i: (0, 0)),
                  pl.BlockSpec((1, D), lambda i: (0, 0)),
                  pl.BlockSpec((N_EXPERTS, D), lambda i: (0, 0))],
        out_specs=[pl.BlockSpec((tr, D), lambda i: (i, 0)),
                   pl.BlockSpec((N_EXPERTS, tr), lambda i: (0, i))],
        out_shape=[jax.ShapeDtypeStruct((T, D), F32), jax.ShapeDtypeStruct((N_EXPERTS, T), F32)],
        compiler_params=_params(("parallel",)),
        name="ln1_router",
    )(z, g.reshape(1, D), b.reshape(1, D), w_router_t)


def _route_kernel(lg_ref, idx_ref, pk_ref, pos_ref, gk_ref, mk_ref, kmax_ref, *, cap, tok0):
    E, R, _ = lg_ref.shape
    logits = lg_ref[...]
    ex = jnp.exp(logits - jnp.max(logits, axis=0, keepdims=True))
    aff = ex / jnp.sum(ex, axis=0, keepdims=True)
    bits = pltpu.bitcast(aff, I32)

    def count(pred):
        c = jnp.sum(jnp.where(pred, 1.0, 0.0), axis=2, keepdims=True)
        return jnp.sum(c, axis=1, keepdims=True)

    capf = jnp.float32(cap)

    def thr_step(i, thr):
        cand = thr | lax.shift_left(jnp.int32(1), 30 - i)
        return jnp.where(count(bits >= cand) >= capf, cand, thr)

    thr = lax.fori_loop(0, 31, thr_step, jnp.zeros((E, 1, 1), I32))
    gt = bits > thr
    eq = bits == thr
    need = capf - count(gt)
    tid = (lax.broadcasted_iota(I32, (E, R, LANES), 1) * LANES
           + lax.broadcasted_iota(I32, (E, R, LANES), 2))
    nbits = int(math.log2(R * LANES))

    def tie_step(i, v):
        cand = v | lax.shift_left(jnp.int32(1), nbits - 1 - i)
        return jnp.where(count(eq & (tid < cand)) < need, cand, v)

    vtie = lax.fori_loop(0, nbits, tie_step, jnp.zeros((E, 1, 1), I32))
    mask = jnp.where(gt | (eq & (tid <= vtie)), 1.0, 0.0)

    ranks = []
    cnt = jnp.zeros((R, LANES), F32)
    for e in range(E):
        ranks.append(cnt)
        cnt = cnt + mask[e]
    for k in range(E):
        gk = jnp.zeros((R, LANES), F32)
        for e in range(k, E):
            gk = gk + jnp.where((ranks[e] == k) & (mask[e] > 0), aff[e], 0.0)
        gk_ref[k] = gk
        mk_ref[k] = jnp.where(cnt > k, 1.0, 0.0)
    kmax_ref[...] = jnp.zeros(kmax_ref.shape, I32)
    for t in range(LANES // COMBINE_TOKENS):
        tile_cnt = cnt[:, t * COMBINE_TOKENS:(t + 1) * COMBINE_TOKENS]
        kmax_ref[:, t:t + 1] = jnp.max(tile_cnt, axis=1, keepdims=True).astype(I32)
    lane_col = lax.broadcasted_iota(I32, (LANES, 1), 0).astype(F32)

    mb = mask.astype(BF16)
    li = lax.broadcasted_iota(I32, (LANES, LANES), 0)
    lj = lax.broadcasted_iota(I32, (LANES, LANES), 1)
    upper = jnp.where(li <= lj, 1.0, 0.0).astype(BF16)
    lower = jnp.where(li >= lj, 1.0, 0.0).astype(BF16)
    ri = lax.broadcasted_iota(I32, (R, R), 0)
    rj = lax.broadcasted_iota(I32, (R, R), 1)
    strict = jnp.where(rj < ri, 1.0, 0.0).astype(BF16)
    slot = lax.broadcasted_iota(I32, (1, cap), 1).astype(F32)
    rcol = lax.broadcasted_iota(I32, (R, 1), 0).astype(F32)
    for e in range(E):
        me = mb[e]
        winc = jnp.dot(me, upper, preferred_element_type=F32)
        rows = winc[:, LANES - 1:LANES]
        offs = jnp.sum(jnp.dot(strict, me, preferred_element_type=F32), axis=-1, keepdims=True)
        pos_ref[e] = (offs + winc - mask[e]).astype(I32)
        ends = offs + rows
        rs = jnp.sum(jnp.where(ends <= slot, 1.0, 0.0), axis=0, keepdims=True)
        onehot = rcol == rs
        offs_s = jnp.sum(jnp.where(onehot, offs, 0.0), axis=0, keepdims=True)
        winc_t = lax.dot_general(lower, me, (((1,), (1,)), ((), ())), preferred_element_type=F32)
        wg = jnp.dot(winc_t.astype(BF16), jnp.where(onehot, 1.0, 0.0).astype(BF16),
                     preferred_element_type=F32)
        ls = jnp.sum(jnp.where(wg <= slot - offs_s, 1.0, 0.0), axis=0, keepdims=True)
        tok = (rs * LANES + ls).astype(I32)
        idx_ref[e:e + 1, :] = tok + tok0
        rg = lax.dot_general(ranks[e].astype(BF16), jnp.where(onehot, 1.0, 0.0).astype(BF16),
                             (((0,), (0,)), ((), ())), preferred_element_type=F32)
        rk = jnp.sum(jnp.where(lane_col == ls, rg, 0.0), axis=0, keepdims=True).astype(I32)
        pk_ref[e:e + 1, :] = tok * E + rk


def _route(logits_t, cap, tok0):
    E, Tg = logits_t.shape
    R = Tg // LANES
    per_tok = jax.ShapeDtypeStruct((E, R, LANES), F32)
    idx, pk, pos, gk, mk, kmax = pl.pallas_call(
        functools.partial(_route_kernel, cap=cap, tok0=tok0),
        out_shape=[jax.ShapeDtypeStruct((E, cap), I32), jax.ShapeDtypeStruct((E, cap), I32),
                   jax.ShapeDtypeStruct((E, R, LANES), I32), per_tok, per_tok,
                   jax.ShapeDtypeStruct((R, LANES), I32)],
        compiler_params=pltpu.CompilerParams(vmem_limit_bytes=V7X_VMEM_LIMIT_BYTES),
        name="route",
    )(logits_t.reshape(E, R, LANES))
    kmax = kmax[:, :LANES // COMBINE_TOKENS].reshape(-1)
    return idx, pk, pos.reshape(E, Tg), gk.reshape(E, Tg), mk.reshape(E, Tg), kmax


def _gather_kernel(idx_ref, idx_next_ref, x_hbm, o_ref, buf, sem):
    i = pl.program_id(0)
    tg = buf.shape[1]

    def row_copy(src_row, slot, dst_row):
        return pltpu.make_async_copy(x_hbm.at[pl.ds(src_row, 1)], buf.at[slot, pl.ds(dst_row, 1)], sem.at[slot])

    def issue(ref, slot):
        def body(r, carry):
            row_copy(ref[0, 0, r], slot, r).start()
            return carry
        lax.fori_loop(0, tg, body, 0, unroll=8)

    @pl.when(i == 0)
    def _():
        issue(idx_ref, 0)

    @pl.when(i + 1 < pl.num_programs(0))
    def _():
        issue(idx_next_ref, (i + 1) % 2)

    slot = i % 2
    for _ in range(tg):
        row_copy(0, slot, 0).wait()
    o_ref[...] = buf[slot].astype(o_ref.dtype)


def _gather_tokens(x1, idx_all):
    T, D = x1.shape
    n_rows = idx_all.size
    tg = _tile(n_rows, 256)
    n_tiles = n_rows // tg
    idx_tiles = idx_all.reshape(n_tiles, 1, tg)
    return pl.pallas_call(
        _gather_kernel,
        grid=(n_tiles,),
        in_specs=[pl.BlockSpec((1, 1, tg), lambda i: (i, 0, 0), memory_space=pltpu.SMEM),
                  pl.BlockSpec((1, 1, tg), lambda i: (jnp.minimum(i + 1, n_tiles - 1), 0, 0),
                               memory_space=pltpu.SMEM),
                  pl.BlockSpec(memory_space=pl.ANY)],
        out_specs=pl.BlockSpec((tg, D), lambda i: (i, 0)),
        out_shape=jax.ShapeDtypeStruct((n_rows, D), BF16),
        scratch_shapes=[pltpu.VMEM((2, tg, D), F32), pltpu.SemaphoreType.DMA((2,))],
        compiler_params=_params(("arbitrary",)),
        name="gather_tokens",
    )(idx_tiles, idx_tiles, x1)


def _ffn_a_kernel(x_ref, wg_ref, wu_ref, h_ref):
    x = x_ref[...]
    g = jnp.dot(x, wg_ref[...].astype(BF16), preferred_element_type=F32)
    u = jnp.dot(x, wu_ref[...].astype(BF16), preferred_element_type=F32)
    h_ref[...] = (g * _sigmoid(g) * u).astype(h_ref.dtype)


def _ffn_hidden(xe, w_gate, w_up):
    E, Ctot, D = xe.shape
    F = w_gate.shape[2]
    tm = _tile(Ctot, 1536)
    tf = _tile(F, 256)
    return pl.pallas_call(
        _ffn_a_kernel,
        grid=(E, Ctot // tm, F // tf),
        in_specs=[pl.BlockSpec((None, tm, D), lambda e, m, f: (e, m, 0)),
                  pl.BlockSpec((None, D, tf), lambda e, m, f: (e, 0, f)),
                  pl.BlockSpec((None, D, tf), lambda e, m, f: (e, 0, f))],
        out_specs=pl.BlockSpec((None, tm, tf), lambda e, m, f: (e, m, f)),
        out_shape=jax.ShapeDtypeStruct((E, Ctot, F), BF16),
        compiler_params=_params(("parallel", "parallel", "arbitrary")),
        name="ffn_hidden",
    )(xe, w_gate, w_up)


def _ffn_b_kernel(h_ref, wd_ref, y_ref):
    @pl.when(pl.program_id(3) == 0)
    def _():
        y_ref[...] = jnp.zeros(y_ref.shape, F32)

    y_ref[...] += jnp.dot(h_ref[...], wd_ref[...].astype(BF16), preferred_element_type=F32)


def _ffn_out(h, w_down):
    E, Ctot, F = h.shape
    D = w_down.shape[2]
    tm = _tile(Ctot, 1536)
    tn = _tile(D, 1024)
    tk = _tile(F, 2048)
    return pl.pallas_call(
        _ffn_b_kernel,
        grid=(E, Ctot // tm, D // tn, F // tk),
        in_specs=[pl.BlockSpec((None, tm, tk), lambda e, m, n, k: (e, m, k)),
                  pl.BlockSpec((None, tk, tn), lambda e, m, n, k: (e, k, n))],
        out_specs=pl.BlockSpec((None, tm, tn), lambda e, m, n, k: (e, m, n)),
        out_shape=jax.ShapeDtypeStruct((E, Ctot, D), F32),
        compiler_params=_params(("parallel", "parallel", "parallel", "arbitrary")),
        name="ffn_out",
    )(h, w_down)


def _combine_kernel(starts_ref, pk_ref, kmax_ref, ye_hbm, gk_ref, mk_ref, x1_ref, g_ref, b_ref, o_ref, buf, sem,
                    *, row0, ctot):
    E = buf.shape[1]
    tt = x1_ref.shape[0]
    j = pl.program_id(0)

    def row_copy(src_row, slot, k, dst_row):
        return pltpu.make_async_copy(ye_hbm.at[pl.ds(src_row, 1)], buf.at[slot, k, pl.ds(dst_row, 1)],
                                     sem.at[slot])

    def issue(tile, slot):
        for e in range(E):
            s0 = starts_ref[e, tile]
            n = starts_ref[e, tile + 1] - s0

            def one(s, e=e):
                word = pk_ref[e, s]
                tok = lax.shift_right_logical(word, int(math.log2(E)))
                row_copy(row0 + e * ctot + s, slot, word & (E - 1), tok - tile * tt).start()

            def body4(i, carry, s0=s0, one=one):
                for u in range(4):
                    one(s0 + 4 * i + u)
                return carry

            def body1(i, carry, s0=s0, n=n, one=one):
                one(s0 + (n & ~3) + i)
                return carry

            lax.fori_loop(0, lax.shift_right_logical(n, 2), body4, 0)
            lax.fori_loop(0, n & 3, body1, 0)

    @pl.when(j == 0)
    def _():
        buf[...] = jnp.zeros(buf.shape, F32)
        issue(0, 0)

    @pl.when(j + 1 < pl.num_programs(0))
    def _():
        issue(j + 1, (j + 1) % 2)

    slot = j % 2
    n_rows = jnp.int32(0)
    for e in range(E):
        n_rows = n_rows + (starts_ref[e, j + 1] - starts_ref[e, j])

    def drain8(r, carry):
        for _ in range(8):
            row_copy(0, slot, 0, 0).wait()
        return carry

    def drain1(r, carry):
        row_copy(0, slot, 0, 0).wait()
        return carry

    lax.fori_loop(0, lax.shift_right_logical(n_rows, 3), drain8, 0)
    lax.fori_loop(0, n_rows & 7, drain1, 0)

    gk = gk_ref[...]
    mk = mk_ref[...]
    n_k = kmax_ref[j]

    def picked(k):
        return jnp.where(mk[:, k:k + 1] > 0, buf[slot, k] * gk[:, k:k + 1], 0.0)

    o_ref[...] = DEEPNORM_ALPHA * x1_ref[...] + (picked(0) + picked(1))
    for k in range(2, E, 2):
        @pl.when(k < n_k)
        def _(k=k):
            o_ref[...] += picked(k) + picked(k + 1)
    o_ref[...] = _layer_norm_rows(o_ref[...], g_ref[...], b_ref[...])


def _combine(ye_rows, starts, pk, kmax, gk_t, mk_t, x1, g, b, tok0, row0, ctot):
    Tg = gk_t.shape[0]
    D = x1.shape[1]
    tt = COMBINE_TOKENS
    blk0 = tok0 // tt
    grid_spec = pltpu.PrefetchScalarGridSpec(
        num_scalar_prefetch=3,
        grid=(Tg // tt,),
        in_specs=[pl.BlockSpec(memory_space=pl.ANY),
                  pl.BlockSpec((tt, N_EXPERTS), lambda j, *_: (j, 0)),
                  pl.BlockSpec((tt, N_EXPERTS), lambda j, *_: (j, 0)),
                  pl.BlockSpec((tt, D), lambda j, *_: (j + blk0, 0)),
                  pl.BlockSpec((1, D), lambda j, *_: (0, 0)),
                  pl.BlockSpec((1, D), lambda j, *_: (0, 0))],
        out_specs=pl.BlockSpec((tt, D), lambda j, *_: (j, 0)),
        scratch_shapes=[pltpu.VMEM((2, N_EXPERTS, tt, D), F32), pltpu.SemaphoreType.DMA((2,))],
    )
    return pl.pallas_call(
        functools.partial(_combine_kernel, row0=row0, ctot=ctot),
        grid_spec=grid_spec,
        out_shape=jax.ShapeDtypeStruct((Tg, D), F32),
        compiler_params=_params(("arbitrary",)),
        name="combine_ln2",
    )(starts, pk, kmax, ye_rows, gk_t, mk_t, x1, g.reshape(1, D), b.reshape(1, D))


def _encoder_layer(x_groups, layer_idx, w_in, w_out, diff_lambda, diff_norm_g, ret_log2_decay, ret_norm_g,
                   ln_g, ln_b, w_router, w_gate, w_up, w_down):
    seq = x_groups[0].shape[1]
    D = x_groups[0].shape[2]
    assert all(x.shape[1] == seq for x in x_groups)
    lam_init = 0.8 - 0.6 * math.exp(-0.3 * layer_idx)
    x2d = [g.reshape(-1, D) for g in x_groups]
    xb = jnp.concatenate([g.astype(BF16) for g in x2d], axis=0)
    wb = w_in.astype(BF16)
    wob = w_out.astype(BF16)

    c128, s128 = _rope_tables(seq, D_DIFF)
    c256, s256 = _rope_tables(seq, DK_RET)
    tab128 = (jnp.concatenate([c128, c128], axis=1), jnp.concatenate([-s128, s128], axis=1))
    tab256 = (c256, s256)

    qkd = _project(xb, wb, 0 * GROUP_W, 2 * GROUP_W, "rot128", BF16, seq, tab128)
    vd = _project(xb, wb, 2 * GROUP_W, GROUP_W, "plain", BF16, seq)
    qkr = _project(xb, wb, 3 * GROUP_W, 2 * GROUP_W, "rot256", BF16, seq, tab256)
    vr = _project(xb, wb, 5 * GROUP_W, GROUP_W, "plain", BF16, seq)
    gr = _project(xb, wb, 6 * GROUP_W, GROUP_W, "plain", F32, seq)

    od = _diff_attention(qkd, vd, diff_lambda, diff_norm_g, seq, lam_init)
    orr = _retention(qkr, vr, gr, ret_log2_decay, ret_norm_g, seq)
    z = _out_projection(x2d, od, orr, wob)
    x1, logits_t = _ln1_router(z, ln_g[0], ln_b[0], w_router.T)

    group_tokens = [g.shape[0] * seq for g in x_groups]
    caps = [min(t, max(1, EC_CAPACITY * t // N_EXPERTS)) for t in group_tokens]
    ctot = sum(caps)
    routed = []
    tok0 = 0
    for tg, cap in zip(group_tokens, caps):
        routed.append(_route(logits_t[:, tok0:tok0 + tg], cap, tok0))
        tok0 += tg
    idx_all = jnp.concatenate([r[0] for r in routed], axis=1)
    xe = _gather_tokens(x1, idx_all).reshape(N_EXPERTS, ctot, D)
    h = _ffn_hidden(xe, w_gate, w_up)
    ye = _ffn_out(h, w_down).reshape(N_EXPERTS * ctot, D)

    outs = []
    tok0 = 0
    row0 = 0
    for g, tg, cap, (_, pk, pos, gk, mk, kmax) in zip(x_groups, group_tokens, caps, routed):
        starts = jnp.concatenate([pos[:, ::COMBINE_TOKENS], jnp.full((N_EXPERTS, 1), cap, I32)], axis=1)
        y = _combine(ye, starts, pk, kmax, gk.T, mk.T, x1, ln_g[1], ln_b[1], tok0, row0, ctot)
        outs.append(y.reshape(g.shape))
        tok0 += tg
        row0 += cap
    return outs


def kernel(x_prompt, x_sample, w_in, w_out, diff_lambda, diff_norm_g, ret_log2_decay, ret_norm_g, ln_g, ln_b,
           w_router, w_gate, w_up, w_down):
    xs = [x_prompt, x_sample]
    for l in range(w_in.shape[0]):
        xs = _encoder_layer(xs, l, w_in[l], w_out[l], diff_lambda[l], diff_norm_g[l], ret_log2_decay[l],
                            ret_norm_g[l], ln_g[l], ln_b[l], w_router[l], w_gate[l], w_up[l], w_down[l])
    return (xs[0], xs[1])
```

```python
import functools
import math

import jax
import jax.numpy as jnp
from jax import lax
from jax.experimental import pallas as pl
from jax.experimental.pallas import tpu as pltpu

F32 = jnp.float32
BF16 = jnp.bfloat16
I32 = jnp.int32

H_DIFF = 8
D_DIFF = 128
H_RET = 8
DK_RET = 256
DV_RET = 256
HEAD_W = 256
GROUP_W = H_DIFF * HEAD_W
N_EXPERTS = 16
EC_CAPACITY = 2
ROPE_THETA = 10000.0
NORM_EPS = 1e-5
DEPTH = 1
DEEPNORM_ALPHA = (2.0 * DEPTH) ** 0.25
ATTN_Q_SCALE = (D_DIFF ** -0.5) * math.log2(math.e)

LANES = 128
V7X_VMEM_LIMIT_BYTES = 56 * 2**20
COMBINE_TOKENS = 64


def _tile(dim, pref):
    t = min(dim, pref)
    while dim % t:
        t //= 2
    return t


def _params(sem):
    return pltpu.CompilerParams(dimension_semantics=sem, vmem_limit_bytes=V7X_VMEM_LIMIT_BYTES)


def _sigmoid(x):
    return 1.0 / (1.0 + jnp.exp(-x))


def _proj_kernel(x_ref, w_ref, *rest, kind, n_q_blocks):
    if kind == "plain":
        (o_ref,) = rest
    else:
        cos_ref, sin_ref, o_ref = rest
    acc = jnp.dot(x_ref[...], w_ref[...], preferred_element_type=F32)
    tn = acc.shape[1]
    if kind == "plain":
        o_ref[...] = acc.astype(o_ref.dtype)
    elif kind == "rot128":
        scale = jnp.where(pl.program_id(1) < n_q_blocks, ATTN_Q_SCALE, 1.0).astype(F32)
        cos = cos_ref[...] * scale
        sin = sin_ref[...] * scale
        for c in range(tn // LANES):
            xc = acc[:, c * LANES:(c + 1) * LANES]
            o_ref[:, c * LANES:(c + 1) * LANES] = (xc * cos + pltpu.roll(xc, 64, 1) * sin).astype(o_ref.dtype)
    else:
        cos = cos_ref[...]
        sin = sin_ref[...]
        scale = jnp.where(pl.program_id(1) >= n_q_blocks, DK_RET ** -0.5, 1.0).astype(F32)
        for c in range(tn // HEAD_W):
            x1 = acc[:, c * HEAD_W:c * HEAD_W + LANES]
            x2 = acc[:, c * HEAD_W + LANES:(c + 1) * HEAD_W]
            o_ref[:, c * HEAD_W:c * HEAD_W + LANES] = ((x1 * cos - x2 * sin) * scale).astype(o_ref.dtype)
            o_ref[:, c * HEAD_W + LANES:(c + 1) * HEAD_W] = ((x2 * cos + x1 * sin) * scale).astype(o_ref.dtype)


def _project(xb, wb, col0, ncols, kind, out_dtype, seq, tables=None):
    T, D = xb.shape
    tm = _tile(seq, 1024)
    tn = _tile(ncols, 1024)
    nb0 = col0 // tn
    in_specs = [pl.BlockSpec((tm, D), lambda i, j: (i, 0)),
                pl.BlockSpec((D, tn), lambda i, j: (0, j + nb0))]
    args = [xb, wb]
    if kind != "plain":
        npos = seq // tm
        in_specs += [pl.BlockSpec((tm, LANES), lambda i, j: (i % npos, 0))] * 2
        args += list(tables)
    return pl.pallas_call(
        functools.partial(_proj_kernel, kind=kind, n_q_blocks=GROUP_W // tn),
        grid=(T // tm, ncols // tn),
        in_specs=in_specs,
        out_specs=pl.BlockSpec((tm, tn), lambda i, j: (i, j)),
        out_shape=jax.ShapeDtypeStruct((T, ncols), out_dtype),
        compiler_params=_params(("parallel", "arbitrary")),
        name="in_proj_" + kind,
    )(*args)


def _rope_tables(seq, d):
    inv_freq = 1.0 / (ROPE_THETA ** (jnp.arange(0, d, 2, dtype=F32) / d))
    ang = jnp.arange(seq, dtype=F32)[:, None] * inv_freq[None, :]
    return jnp.cos(ang), jnp.sin(ang)


ATTN_ROW_CHUNK = 16


def _attn_kernel(q_ref, k_ref, v_ref, lam_ref, g_ref, o_ref, m_ref, l_ref, mp_ref, acc_ref, s_ref, p_ref,
                 *, tk, lam_init):
    seq = k_ref.shape[0]
    tq = q_ref.shape[0]
    rc = min(ATTN_ROW_CHUNK, tq)
    m_ref[...] = jnp.full(m_ref.shape, -jnp.inf, F32)
    l_ref[...] = jnp.zeros(l_ref.shape, F32)
    acc_ref[...] = jnp.zeros(acc_ref.shape, F32)

    nt = tk // LANES

    def lane_fold(x, op):
        y = x[:, :LANES]
        for t in range(1, nt):
            y = op(y, x[:, t * LANES:(t + 1) * LANES])
        return y

    def body(i, carry):
        off = pl.multiple_of(i * tk, tk)
        for c in range(2):
            s_ref[c] = lax.dot_general(q_ref[:, c * D_DIFF:(c + 1) * D_DIFF],
                                       k_ref[pl.ds(off, tk), c * D_DIFF:(c + 1) * D_DIFF],
                                       (((1,), (1,)), ((), ())), preferred_element_type=F32)
        for c in range(2):
            for r in range(tq // rc):
                rows = slice(r * rc, (r + 1) * rc)
                mp_ref[rows, :] = lane_fold(s_ref[c, rows, :], jnp.maximum)
            m_old = m_ref[c]
            m_new = jnp.maximum(m_old, jnp.max(mp_ref[...], axis=-1, keepdims=True))
            alpha = jnp.exp2(m_old - m_new)
            m_ref[c] = m_new
            for r in range(tq // rc):
                rows = slice(r * rc, (r + 1) * rc)
                mrow = m_ref[c, rows, :]
                p = jnp.exp2(s_ref[c, rows, :] - jnp.concatenate([mrow] * nt, axis=1))
                mp_ref[rows, :] = lane_fold(p, jnp.add)
                p_ref[c, rows, :] = p.astype(BF16)
            l_ref[c] = alpha * l_ref[c] + mp_ref[...]
            acc_ref[c] = (jnp.concatenate([alpha] * (HEAD_W // LANES), axis=1) * acc_ref[c]
                          + jnp.dot(p_ref[c], v_ref[pl.ds(off, tk), :], preferred_element_type=F32))
        return carry

    lax.fori_loop(0, seq // tk, body, 0)

    dl = lam_ref[...]
    lam = (jnp.exp(jnp.sum(dl[0:1] * dl[1:2], axis=-1, keepdims=True))
           - jnp.exp(jnp.sum(dl[2:3] * dl[3:4], axis=-1, keepdims=True)) + lam_init)
    l0 = jnp.sum(l_ref[0], axis=-1, keepdims=True)
    l1 = jnp.sum(l_ref[1], axis=-1, keepdims=True)
    o = acc_ref[0] / l0 - lam * (acc_ref[1] / l1)
    o = o * lax.rsqrt(jnp.mean(o * o, axis=-1, keepdims=True) + NORM_EPS)
    o_ref[...] = (o * g_ref[...] * (1.0 - lam_init)).astype(o_ref.dtype)


def _diff_attention(qk, v, diff_lambda, diff_norm_g, seq, lam_init):
    T = v.shape[0]
    nb = T // seq
    tq = _tile(seq, 1024)
    tk = _tile(seq, 1024)
    nq = seq // tq
    return pl.pallas_call(
        functools.partial(_attn_kernel, tk=tk, lam_init=lam_init),
        grid=(nb, H_DIFF, nq),
        in_specs=[pl.BlockSpec((tq, HEAD_W), lambda b, h, i: (b * nq + i, h)),
                  pl.BlockSpec((seq, HEAD_W), lambda b, h, i: (b, H_DIFF + h)),
                  pl.BlockSpec((seq, HEAD_W), lambda b, h, i: (b, h)),
                  pl.BlockSpec((4, D_DIFF), lambda b, h, i: (0, 0)),
                  pl.BlockSpec((1, HEAD_W), lambda b, h, i: (0, 0))],
        out_specs=pl.BlockSpec((tq, HEAD_W), lambda b, h, i: (b * nq + i, h)),
        out_shape=jax.ShapeDtypeStruct((T, GROUP_W), BF16),
        scratch_shapes=[pltpu.VMEM((2, tq, LANES), F32), pltpu.VMEM((2, tq, LANES), F32),
                        pltpu.VMEM((tq, LANES), F32),
                        pltpu.VMEM((2, tq, HEAD_W), F32), pltpu.VMEM((2, tq, tk), F32),
                        pltpu.VMEM((2, tq, tk), BF16)],
        compiler_params=_params(("parallel", "parallel", "arbitrary")),
        name="diff_attention",
    )(qk, qk, v, diff_lambda, diff_norm_g.reshape(1, HEAD_W))


RET_HEADS_PER_STEP = 2


def _ret_kernel(rate_ref, qf_ref, kf_ref, vf_ref, gf_ref, qb_ref, kb_ref, vb_ref, gb_ref, gn_ref,
                o_ref, st_ref, osum_ref, dec_ref, tab_ref, car_ref, *, cc, n):
    hps = RET_HEADS_PER_STEP
    h0 = (pl.program_id(0) % (H_RET // hps)) * hps
    c = pl.program_id(1)

    @pl.when(c == 0)
    def _():
        st_ref[...] = jnp.zeros(st_ref.shape, F32)
        ii = lax.broadcasted_iota(I32, (cc, cc), 0).astype(F32)
        jj = lax.broadcasted_iota(I32, (cc, cc), 1).astype(F32)
        dlt = ii - jj
        pos = lax.broadcasted_iota(I32, (cc, HEAD_W), 0).astype(F32)
        for hh in range(hps):
            def log_gamma(d, hh=hh):
                rate = jnp.full((1, 1), rate_ref[d, h0 + hh], F32)
                return jnp.log1p(-jnp.exp2(rate))

            lgf = log_gamma(0)
            lgb = log_gamma(1)
            dec_ref[hh, 0] = jnp.where(dlt >= 0, jnp.exp(jnp.where(dlt >= 0, dlt, 0.0) * lgf), 0.0)
            dec_ref[hh, 1] = jnp.where(dlt < 0, jnp.exp(jnp.where(dlt < 0, -dlt, 0.0) * lgb), 0.0)
            tab_ref[hh, 0] = jnp.exp((pos + 1.0) * lgf)
            tab_ref[hh, 1] = jnp.exp((cc - 1.0 - pos) * lgf)
            tab_ref[hh, 2] = jnp.exp((cc - pos) * lgb)
            tab_ref[hh, 3] = jnp.exp(pos * lgb)
            car_ref[hh, 0] = jnp.broadcast_to(jnp.exp(cc * lgf), car_ref.shape[2:])
            car_ref[hh, 1] = jnp.broadcast_to(jnp.exp(cc * lgb), car_ref.shape[2:])

    def direction(q_ref, k_ref, v_ref, hh, d):
        cols = slice(hh * HEAD_W, (hh + 1) * HEAD_W)
        q = q_ref[:, cols]
        k = k_ref[:, cols]
        v = v_ref[:, cols]
        s = lax.dot_general(q, k, (((1,), (1,)), ((), ())), preferred_element_type=F32) * dec_ref[hh, d]
        state = st_ref[hh, d]
        o = (jnp.dot(s.astype(BF16), v, preferred_element_type=F32)
             + jnp.dot(q, state.astype(BF16), preferred_element_type=F32) * tab_ref[hh, 2 * d])
        kz = (k.astype(F32) * tab_ref[hh, 2 * d + 1]).astype(BF16)
        st_ref[hh, d] = car_ref[hh, d, 0:1, 0:1] * state + lax.dot_general(
            kz, v, (((0,), (0,)), ((), ())), preferred_element_type=F32)
        return o

    rows_f = pl.ds(pl.multiple_of(c * cc, cc), cc)
    rows_b = pl.ds(pl.multiple_of((n - 1 - c) * cc, cc), cc)

    def finish(o, gate_ref, cols):
        mu = jnp.mean(o, axis=-1, keepdims=True)
        var = jnp.mean(jnp.square(o - mu), axis=-1, keepdims=True)
        y = (o - mu) * lax.rsqrt(var + NORM_EPS) * gn_ref[:, cols]
        gate = gate_ref[:, cols]
        return (y * (gate * _sigmoid(gate))).astype(o_ref.dtype)

    outs = [(direction(qf_ref, kf_ref, vf_ref, hh, 0), direction(qb_ref, kb_ref, vb_ref, hh, 1))
            for hh in range(hps)]

    @pl.when(c < n // 2)
    def _():
        for hh, (o_f, o_b) in enumerate(outs):
            cols = slice(hh * HEAD_W, (hh + 1) * HEAD_W)
            osum_ref[rows_f, cols] = o_f
            osum_ref[rows_b, cols] = o_b

    @pl.when(c >= n // 2)
    def _():
        for hh, (o_f, o_b) in enumerate(outs):
            cols = slice(hh * HEAD_W, (hh + 1) * HEAD_W)
            o_ref[rows_f, cols] = finish(osum_ref[rows_f, cols] + o_f, gf_ref, cols)
            o_ref[rows_b, cols] = finish(osum_ref[rows_b, cols] + o_b, gb_ref, cols)


def _retention(qk, v, gate, ret_log2_decay, ret_norm_g, seq):
    T = v.shape[0]
    nb = T // seq
    cc = _tile(seq // 2, 256)
    n = seq // cc
    assert n % 2 == 0

    hps = RET_HEADS_PER_STEP
    hg = H_RET // hps
    bw = hps * HEAD_W

    def fwd(col):
        return pl.BlockSpec((cc, bw), lambda bh, c: ((bh // hg) * n + c, (bh % hg) + col))

    def bwd(col):
        return pl.BlockSpec((cc, bw), lambda bh, c: ((bh // hg) * n + n - 1 - c, (bh % hg) + col))

    return pl.pallas_call(
        functools.partial(_ret_kernel, cc=cc, n=n),
        grid=(nb * hg, n),
        in_specs=[pl.BlockSpec(memory_space=pltpu.SMEM),
                  fwd(0), fwd(hg), fwd(0), fwd(0), bwd(0), bwd(hg), bwd(0), bwd(0),
                  pl.BlockSpec((1, bw), lambda bh, c: (0, bh % hg))],
        out_specs=pl.BlockSpec((seq, bw), lambda bh, c: (bh // hg, bh % hg)),
        out_shape=jax.ShapeDtypeStruct((T, GROUP_W), BF16),
        scratch_shapes=[pltpu.VMEM((hps, 2, DK_RET, DV_RET), F32),
                        pltpu.VMEM((seq, bw), F32), pltpu.VMEM((hps, 2, cc, cc), F32),
                        pltpu.VMEM((hps, 4, cc, HEAD_W), F32), pltpu.VMEM((hps, 2, 8, LANES), F32)],
        compiler_params=_params(("parallel", "arbitrary")),
        name="retention",
    )(ret_log2_decay, qk, qk, v, gate, qk, qk, v, gate, ret_norm_g.reshape(1, GROUP_W))


def _outproj_kernel(a1_ref, a2_ref, w1_ref, w2_ref, o_ref):
    o_ref[...] = (jnp.dot(a1_ref[...], w1_ref[...], preferred_element_type=F32)
                  + jnp.dot(a2_ref[...], w2_ref[...], preferred_element_type=F32))


def _out_projection(od, orr, wob):
    T = od.shape[0]
    D = wob.shape[1]
    tm = _tile(T, 1024)
    tn = _tile(D, 1024)
    return pl.pallas_call(
        _outproj_kernel,
        grid=(T // tm, D // tn),
        in_specs=[pl.BlockSpec((tm, GROUP_W), lambda i, j: (i, 0)),
                  pl.BlockSpec((tm, GROUP_W), lambda i, j: (i, 0)),
                  pl.BlockSpec((GROUP_W, tn), lambda i, j: (0, j)),
                  pl.BlockSpec((GROUP_W, tn), lambda i, j: (1, j))],
        out_specs=pl.BlockSpec((tm, tn), lambda i, j: (i, j)),
        out_shape=jax.ShapeDtypeStruct((T, D), F32),
        compiler_params=_params(("parallel", "arbitrary")),
        name="out_proj",
    )(od, orr, wob, wob)


def _layer_norm_rows(z, g, b):
    mu = jnp.mean(z, axis=-1, keepdims=True)
    var = jnp.mean(jnp.square(z - mu), axis=-1, keepdims=True)
    return (z - mu) * lax.rsqrt(var + NORM_EPS) * g + b


def _ln1_kernel(*refs, tile_starts):
    ng = len(tile_starts) - 1
    x_refs = refs[:ng]
    mix_ref, g_ref, b_ref, wr_ref, x1_ref, lg_ref = refs[ng:]
    i = pl.program_id(0)
    for gi, x_ref in enumerate(x_refs):
        @pl.when((i >= tile_starts[gi]) & (i < tile_starts[gi + 1]))
        def _(x_ref=x_ref):
            y = _layer_norm_rows(DEEPNORM_ALPHA * x_ref[...] + mix_ref[...], g_ref[...], b_ref[...])
            x1_ref[...] = y
            lg_ref[...] = lax.dot_general(wr_ref[...], y, (((1,), (1,)), ((), ())),
                                          precision=lax.Precision.HIGHEST, preferred_element_type=F32)


def _ln1_router(x_groups2d, mix, g, b, w_router_t):
    T, D = mix.shape
    tr = _tile(math.gcd(*[x.shape[0] for x in x_groups2d]), 256)
    tile_starts = [0]
    for x in x_groups2d:
        tile_starts.append(tile_starts[-1] + x.shape[0] // tr)

    def x_spec(gi):
        lo, hi = tile_starts[gi], tile_starts[gi + 1]
        return pl.BlockSpec((tr, D), lambda i: (jnp.clip(i - lo, 0, hi - lo - 1), 0))

    return pl.pallas_call(
        functools.partial(_ln1_kernel, tile_starts=tuple(tile_starts)),
        grid=(T // tr,),
        in_specs=[x_spec(gi) for gi in range(len(x_groups2d))] + [
                  pl.BlockSpec((tr, D), lambda i: (i, 0)),
                  pl.BlockSpec((1, D), lambda i: (0, 0)),
                  pl.BlockSpec((1, D), lambda i: (0, 0)),
                  pl.BlockSpec((N_EXPERTS, D), lambda i: (0, 0))],
        out_specs=[pl.BlockSpec((tr, D), lambda i: (i, 0)),
                   pl.BlockSpec((N_EXPERTS, tr), lambda i: (0, i))],
        out_shape=[jax.ShapeDtypeStruct((T, D), F32), jax.ShapeDtypeStruct((N_EXPERTS, T), F32)],
        compiler_params=_params(("parallel",)),
        name="ln1_router",
    )(*x_groups2d, mix, g.reshape(1, D), b.reshape(1, D), w_router_t)


def _route_kernel(lg_ref, idx_ref, pk_ref, pos_ref, gk_ref, mk_ref, kmax_ref, *, cap, tok0):
    E, R, _ = lg_ref.shape
    logits = lg_ref[...]
    ex = jnp.exp(logits - jnp.max(logits, axis=0, keepdims=True))
    aff = ex / jnp.sum(ex, axis=0, keepdims=True)
    bits = pltpu.bitcast(aff, I32)

    def count(pred):
        c = jnp.sum(jnp.where(pred, 1.0, 0.0), axis=2, keepdims=True)
        return jnp.sum(c, axis=1, keepdims=True)

    capf = jnp.float32(cap)

    def thr_step(i, thr):
        cand = thr | lax.shift_left(jnp.int32(1), 30 - i)
        return jnp.where(count(bits >= cand) >= capf, cand, thr)

    thr = lax.fori_loop(0, 31, thr_step, jnp.zeros((E, 1, 1), I32))
    gt = bits > thr
    eq = bits == thr
    need = capf - count(gt)
    tid = (lax.broadcasted_iota(I32, (E, R, LANES), 1) * LANES
           + lax.broadcasted_iota(I32, (E, R, LANES), 2))
    nbits = int(math.log2(R * LANES))

    def tie_step(i, v):
        cand = v | lax.shift_left(jnp.int32(1), nbits - 1 - i)
        return jnp.where(count(eq & (tid < cand)) < need, cand, v)

    vtie = lax.fori_loop(0, nbits, tie_step, jnp.zeros((E, 1, 1), I32))
    mask = jnp.where(gt | (eq & (tid <= vtie)), 1.0, 0.0)

    ranks = []
    cnt = jnp.zeros((R, LANES), F32)
    for e in range(E):
        ranks.append(cnt)
        cnt = cnt + mask[e]
    for k in range(E):
        gk = jnp.zeros((R, LANES), F32)
        for e in range(k, E):
            gk = gk + jnp.where((ranks[e] == k) & (mask[e] > 0), aff[e], 0.0)
        gk_ref[k] = gk
        mk_ref[k] = jnp.where(cnt > k, 1.0, 0.0)
    kmax_ref[...] = jnp.zeros(kmax_ref.shape, I32)
    for t in range(LANES // COMBINE_TOKENS):
        tile_cnt = cnt[:, t * COMBINE_TOKENS:(t + 1) * COMBINE_TOKENS]
        kmax_ref[:, t:t + 1] = jnp.max(tile_cnt, axis=1, keepdims=True).astype(I32)
    lane_col = lax.broadcasted_iota(I32, (LANES, 1), 0).astype(F32)

    mb = mask.astype(BF16)
    li = lax.broadcasted_iota(I32, (LANES, LANES), 0)
    lj = lax.broadcasted_iota(I32, (LANES, LANES), 1)
    upper = jnp.where(li <= lj, 1.0, 0.0).astype(BF16)
    lower = jnp.where(li >= lj, 1.0, 0.0).astype(BF16)
    ri = lax.broadcasted_iota(I32, (R, R), 0)
    rj = lax.broadcasted_iota(I32, (R, R), 1)
    strict = jnp.where(rj < ri, 1.0, 0.0).astype(BF16)
    slot = lax.broadcasted_iota(I32, (1, cap), 1).astype(F32)
    rcol = lax.broadcasted_iota(I32, (R, 1), 0).astype(F32)
    for e in range(E):
        me = mb[e]
        winc = jnp.dot(me, upper, preferred_element_type=F32)
        rows = winc[:, LANES - 1:LANES]
        offs = jnp.sum(jnp.dot(strict, me, preferred_element_type=F32), axis=-1, keepdims=True)
        pos_ref[e] = (offs + winc - mask[e]).astype(I32)
        ends = offs + rows
        rs = jnp.sum(jnp.where(ends <= slot, 1.0, 0.0), axis=0, keepdims=True)
        onehot = rcol == rs
        offs_s = jnp.sum(jnp.where(onehot, offs, 0.0), axis=0, keepdims=True)
        winc_t = lax.dot_general(lower, me, (((1,), (1,)), ((), ())), preferred_element_type=F32)
        wg = jnp.dot(winc_t.astype(BF16), jnp.where(onehot, 1.0, 0.0).astype(BF16),
                     preferred_element_type=F32)
        ls = jnp.sum(jnp.where(wg <= slot - offs_s, 1.0, 0.0), axis=0, keepdims=True)
        tok = (rs * LANES + ls).astype(I32)
        idx_ref[e:e + 1, :] = tok + tok0
        rg = lax.dot_general(ranks[e].astype(BF16), jnp.where(onehot, 1.0, 0.0).astype(BF16),
                             (((0,), (0,)), ((), ())), preferred_element_type=F32)
        rk = jnp.sum(jnp.where(lane_col == ls, rg, 0.0), axis=0, keepdims=True).astype(I32)
        pk_ref[e:e + 1, :] = tok * E + rk


def _route(logits_t, cap, tok0):
    E, Tg = logits_t.shape
    R = Tg // LANES
    per_tok = jax.ShapeDtypeStruct((E, R, LANES), F32)
    idx, pk, pos, gk, mk, kmax = pl.pallas_call(
        functools.partial(_route_kernel, cap=cap, tok0=tok0),
        out_shape=[jax.ShapeDtypeStruct((E, cap), I32), jax.ShapeDtypeStruct((E, cap), I32),
                   jax.ShapeDtypeStruct((E, R, LANES), I32), per_tok, per_tok,
                   jax.ShapeDtypeStruct((R, LANES), I32)],
        compiler_params=pltpu.CompilerParams(vmem_limit_bytes=V7X_VMEM_LIMIT_BYTES),
        name="route",
    )(logits_t.reshape(E, R, LANES))
    kmax = kmax[:, :LANES // COMBINE_TOKENS].reshape(-1)
    return idx, pk, pos.reshape(E, Tg), gk.reshape(E, Tg), mk.reshape(E, Tg), kmax


def _gather_kernel(idx_ref, idx_next_ref, x_hbm, o_ref, buf, sem):
    i = pl.program_id(0)
    tg = buf.shape[1]

    def row_copy(src_row, slot, dst_row):
        return pltpu.make_async_copy(x_hbm.at[pl.ds(src_row, 1)], buf.at[slot, pl.ds(dst_row, 1)], sem.at[slot])

    def issue(ref, slot):
        def body(r, carry):
            row_copy(ref[0, 0, r], slot, r).start()
            return carry
        lax.fori_loop(0, tg, body, 0, unroll=8)

    @pl.when(i == 0)
    def _():
        issue(idx_ref, 0)

    @pl.when(i + 1 < pl.num_programs(0))
    def _():
        issue(idx_next_ref, (i + 1) % 2)

    slot = i % 2
    for _ in range(tg):
        row_copy(0, slot, 0).wait()
    o_ref[...] = buf[slot].astype(o_ref.dtype)


def _gather_tokens(x1, idx_all):
    T, D = x1.shape
    n_rows = idx_all.size
    tg = _tile(n_rows, 256)
    n_tiles = n_rows // tg
    idx_tiles = idx_all.reshape(n_tiles, 1, tg)
    return pl.pallas_call(
        _gather_kernel,
        grid=(n_tiles,),
        in_specs=[pl.BlockSpec((1, 1, tg), lambda i: (i, 0, 0), memory_space=pltpu.SMEM),
                  pl.BlockSpec((1, 1, tg), lambda i: (jnp.minimum(i + 1, n_tiles - 1), 0, 0),
                               memory_space=pltpu.SMEM),
                  pl.BlockSpec(memory_space=pl.ANY)],
        out_specs=pl.BlockSpec((tg, D), lambda i: (i, 0)),
        out_shape=jax.ShapeDtypeStruct((n_rows, D), BF16),
        scratch_shapes=[pltpu.VMEM((2, tg, D), F32), pltpu.SemaphoreType.DMA((2,))],
        compiler_params=_params(("arbitrary",)),
        name="gather_tokens",
    )(idx_tiles, idx_tiles, x1)


def _ffn_a_kernel(x_ref, wg_ref, wu_ref, h_ref):
    x = x_ref[...]
    g = jnp.dot(x, wg_ref[...].astype(BF16), preferred_element_type=F32)
    u = jnp.dot(x, wu_ref[...].astype(BF16), preferred_element_type=F32)
    h_ref[...] = (g * _sigmoid(g) * u).astype(h_ref.dtype)


def _ffn_hidden(xe, w_gate, w_up):
    E, Ctot, D = xe.shape
    F = w_gate.shape[2]
    tm = _tile(Ctot, 1536)
    tf = _tile(F, 256)
    return pl.pallas_call(
        _ffn_a_kernel,
        grid=(E, Ctot // tm, F // tf),
        in_specs=[pl.BlockSpec((None, tm, D), lambda e, m, f: (e, m, 0)),
                  pl.BlockSpec((None, D, tf), lambda e, m, f: (e, 0, f)),
                  pl.BlockSpec((None, D, tf), lambda e, m, f: (e, 0, f))],
        out_specs=pl.BlockSpec((None, tm, tf), lambda e, m, f: (e, m, f)),
        out_shape=jax.ShapeDtypeStruct((E, Ctot, F), BF16),
        compiler_params=_params(("parallel", "parallel", "arbitrary")),
        name="ffn_hidden",
    )(xe, w_gate, w_up)


def _ffn_b_kernel(h_ref, wd_ref, y_ref):
    @pl.when(pl.program_id(3) == 0)
    def _():
        y_ref[...] = jnp.zeros(y_ref.shape, F32)

    y_ref[...] += jnp.dot(h_ref[...], wd_ref[...].astype(BF16), preferred_element_type=F32)


def _ffn_out(h, w_down):
    E, Ctot, F = h.shape
    D = w_down.shape[2]
    tm = _tile(Ctot, 1536)
    tn = _tile(D, 1024)
    tk = _tile(F, 2048)
    return pl.pallas_call(
        _ffn_b_kernel,
        grid=(E, Ctot // tm, D // tn, F // tk),
        in_specs=[pl.BlockSpec((None, tm, tk), lambda e, m, n, k: (e, m, k)),
                  pl.BlockSpec((None, tk, tn), lambda e, m, n, k: (e, k, n))],
        out_specs=pl.BlockSpec((None, tm, tn), lambda e, m, n, k: (e, m, n)),
        out_shape=jax.ShapeDtypeStruct((E, Ctot, D), F32),
        compiler_params=_params(("parallel", "parallel", "parallel", "arbitrary")),
        name="ffn_out",
    )(h, w_down)


def _combine_kernel(starts_ref, pk_ref, kmax_ref, ye_hbm, gk_ref, mk_ref, x1_ref, g_ref, b_ref, o_ref, buf, sem,
                    *, row0, ctot):
    E = buf.shape[1]
    tt = x1_ref.shape[0]
    j = pl.program_id(0)

    def row_copy(src_row, slot, k, dst_row):
        return pltpu.make_async_copy(ye_hbm.at[pl.ds(src_row, 1)], buf.at[slot, k, pl.ds(dst_row, 1)],
                                     sem.at[slot])

    def issue(tile, slot):
        for e in range(E):
            s0 = starts_ref[e, tile]
            n = starts_ref[e, tile + 1] - s0

            def one(s, e=e):
                word = pk_ref[e, s]
                tok = lax.shift_right_logical(word, int(math.log2(E)))
                row_copy(row0 + e * ctot + s, slot, word & (E - 1), tok - tile * tt).start()

            def body4(i, carry, s0=s0, one=one):
                for u in range(4):
                    one(s0 + 4 * i + u)
                return carry

            def body1(i, carry, s0=s0, n=n, one=one):
                one(s0 + (n & ~3) + i)
                return carry

            lax.fori_loop(0, lax.shift_right_logical(n, 2), body4, 0)
            lax.fori_loop(0, n & 3, body1, 0)

    @pl.when(j == 0)
    def _():
        buf[...] = jnp.zeros(buf.shape, F32)
        issue(0, 0)

    @pl.when(j + 1 < pl.num_programs(0))
    def _():
        issue(j + 1, (j + 1) % 2)

    slot = j % 2
    n_rows = jnp.int32(0)
    for e in range(E):
        n_rows = n_rows + (starts_ref[e, j + 1] - starts_ref[e, j])

    def drain8(r, carry):
        for _ in range(8):
            row_copy(0, slot, 0, 0).wait()
        return carry

    def drain1(r, carry):
        row_copy(0, slot, 0, 0).wait()
        return carry

    lax.fori_loop(0, lax.shift_right_logical(n_rows, 3), drain8, 0)
    lax.fori_loop(0, n_rows & 7, drain1, 0)

    gk = gk_ref[...]
    mk = mk_ref[...]
    n_k = kmax_ref[j]

    def picked(k):
        return jnp.where(mk[:, k:k + 1] > 0, buf[slot, k] * gk[:, k:k + 1], 0.0)

    o_ref[...] = DEEPNORM_ALPHA * x1_ref[...] + (picked(0) + picked(1))
    for k in range(2, E, 2):
        @pl.when(k < n_k)
        def _(k=k):
            o_ref[...] += picked(k) + picked(k + 1)
    o_ref[...] = _layer_norm_rows(o_ref[...], g_ref[...], b_ref[...])


def _combine(ye_rows, starts, pk, kmax, gk_t, mk_t, x1, g, b, tok0, row0, ctot):
    Tg = gk_t.shape[0]
    D = x1.shape[1]
    tt = COMBINE_TOKENS
    blk0 = tok0 // tt
    grid_spec = pltpu.PrefetchScalarGridSpec(
        num_scalar_prefetch=3,
        grid=(Tg // tt,),
        in_specs=[pl.BlockSpec(memory_space=pl.ANY),
                  pl.BlockSpec((tt, N_EXPERTS), lambda j, *_: (j, 0)),
                  pl.BlockSpec((tt, N_EXPERTS), lambda j, *_: (j, 0)),
                  pl.BlockSpec((tt, D), lambda j, *_: (j + blk0, 0)),
                  pl.BlockSpec((1, D), lambda j, *_: (0, 0)),
                  pl.BlockSpec((1, D), lambda j, *_: (0, 0))],
        out_specs=pl.BlockSpec((tt, D), lambda j, *_: (j, 0)),
        scratch_shapes=[pltpu.VMEM((2, N_EXPERTS, tt, D), F32), pltpu.SemaphoreType.DMA((2,))],
    )
    return pl.pallas_call(
        functools.partial(_combine_kernel, row0=row0, ctot=ctot),
        grid_spec=grid_spec,
        out_shape=jax.ShapeDtypeStruct((Tg, D), F32),
        compiler_params=_params(("arbitrary",)),
        name="combine_ln2",
    )(starts, pk, kmax, ye_rows, gk_t, mk_t, x1, g.reshape(1, D), b.reshape(1, D))


def _encoder_layer(x_groups, layer_idx, w_in, w_out, diff_lambda, diff_norm_g, ret_log2_decay, ret_norm_g,
                   ln_g, ln_b, w_router, w_gate, w_up, w_down):
    seq = x_groups[0].shape[1]
    D = x_groups[0].shape[2]
    assert all(x.shape[1] == seq for x in x_groups)
    lam_init = 0.8 - 0.6 * math.exp(-0.3 * layer_idx)
    x2d = [g.reshape(-1, D) for g in x_groups]
    xb = jnp.concatenate([g.astype(BF16) for g in x2d], axis=0)
    wb = w_in.astype(BF16)
    wob = w_out.astype(BF16)

    c128, s128 = _rope_tables(seq, D_DIFF)
    c256, s256 = _rope_tables(seq, DK_RET)
    tab128 = (jnp.concatenate([c128, c128], axis=1), jnp.concatenate([-s128, s128], axis=1))
    tab256 = (c256, s256)

    qkd = _project(xb, wb, 0 * GROUP_W, 2 * GROUP_W, "rot128", BF16, seq, tab128)
    vd = _project(xb, wb, 2 * GROUP_W, GROUP_W, "plain", BF16, seq)
    qkr = _project(xb, wb, 3 * GROUP_W, 2 * GROUP_W, "rot256", BF16, seq, tab256)
    vr = _project(xb, wb, 5 * GROUP_W, GROUP_W, "plain", BF16, seq)
    gr = _project(xb, wb, 6 * GROUP_W, GROUP_W, "plain", F32, seq)

    od = _diff_attention(qkd, vd, diff_lambda, diff_norm_g, seq, lam_init)
    orr = _retention(qkr, vr, gr, ret_log2_decay, ret_norm_g, seq)
    mix = _out_projection(od, orr, wob)
    x1, logits_t = _ln1_router(x2d, mix, ln_g[0], ln_b[0], w_router.T)

    group_tokens = [g.shape[0] * seq for g in x_groups]
    caps = [min(t, max(1, EC_CAPACITY * t // N_EXPERTS)) for t in group_tokens]
    ctot = sum(caps)
    routed = []
    tok0 = 0
    for tg, cap in zip(group_tokens, caps):
        routed.append(_route(logits_t[:, tok0:tok0 + tg], cap, tok0))
        tok0 += tg
    idx_all = jnp.concatenate([r[0] for r in routed], axis=1)
    xe = _gather_tokens(x1, idx_all).reshape(N_EXPERTS, ctot, D)
    h = _ffn_hidden(xe, w_gate, w_up)
    ye = _ffn_out(h, w_down).reshape(N_EXPERTS * ctot, D)

    outs = []
    tok0 = 0
    row0 = 0
    for g, tg, cap, (_, pk, pos, gk, mk, kmax) in zip(x_groups, group_tokens, caps, routed):
        starts = jnp.concatenate([pos[:, ::COMBINE_TOKENS], jnp.full((N_EXPERTS, 1), cap, I32)], axis=1)
        y = _combine(ye, starts, pk, kmax, gk.T, mk.T, x1, ln_g[1], ln_b[1], tok0, row0, ctot)
        outs.append(y.reshape(g.shape))
        tok0 += tg
        row0 += cap
    return outs


def kernel(x_prompt, x_sample, w_in, w_out, diff_lambda, diff_norm_g, ret_log2_decay, ret_norm_g, ln_g, ln_b,
           w_router, w_gate, w_up, w_down):
    xs = [x_prompt, x_sample]
    for l in range(w_in.shape[0]):
        xs = _encoder_layer(xs, l, w_in[l], w_out[l], diff_lambda[l], diff_norm_g[l], ret_log2_decay[l],
                            ret_norm_g[l], ln_g[l], ln_b[l], w_router[l], w_gate[l], w_up[l], w_down[l])
    return (xs[0], xs[1])
```

```python
import functools
import math

import jax
import jax.numpy as jnp
from jax import lax
from jax.experimental import pallas as pl
from jax.experimental.pallas import tpu as pltpu

F32 = jnp.float32
BF16 = jnp.bfloat16
I32 = jnp.int32

H_DIFF = 8
D_DIFF = 128
H_RET = 8
DK_RET = 256
DV_RET = 256
HEAD_W = 256
GROUP_W = H_DIFF * HEAD_W
N_EXPERTS = 16
EC_CAPACITY = 2
ROPE_THETA = 10000.0
NORM_EPS = 1e-5
DEPTH = 1
DEEPNORM_ALPHA = (2.0 * DEPTH) ** 0.25
ATTN_Q_SCALE = (D_DIFF ** -0.5) * math.log2(math.e)

LANES = 128
V7X_VMEM_LIMIT_BYTES = 56 * 2**20
COMBINE_TOKENS = 64


def _tile(dim, pref):
    t = min(dim, pref)
    while dim % t:
        t //= 2
    return t


def _params(sem):
    return pltpu.CompilerParams(dimension_semantics=sem, vmem_limit_bytes=V7X_VMEM_LIMIT_BYTES)


def _sigmoid(x):
    return 1.0 / (1.0 + jnp.exp(-x))


def _proj_kernel(x_ref, w_ref, *rest, kind, n_q_blocks):
    if kind == "plain":
        (o_ref,) = rest
    else:
        cos_ref, sin_ref, o_ref = rest
    acc = jnp.dot(x_ref[...], w_ref[...], preferred_element_type=F32)
    tn = acc.shape[1]
    if kind == "plain":
        o_ref[...] = acc.astype(o_ref.dtype)
    elif kind == "rot128":
        scale = jnp.where(pl.program_id(1) < n_q_blocks, ATTN_Q_SCALE, 1.0).astype(F32)
        cos = cos_ref[...] * scale
        sin = sin_ref[...] * scale
        for c in range(tn // LANES):
            xc = acc[:, c * LANES:(c + 1) * LANES]
            o_ref[:, c * LANES:(c + 1) * LANES] = (xc * cos + pltpu.roll(xc, 64, 1) * sin).astype(o_ref.dtype)
    else:
        cos = cos_ref[...]
        sin = sin_ref[...]
        scale = jnp.where(pl.program_id(1) >= n_q_blocks, DK_RET ** -0.5, 1.0).astype(F32)
        for c in range(tn // HEAD_W):
            x1 = acc[:, c * HEAD_W:c * HEAD_W + LANES]
            x2 = acc[:, c * HEAD_W + LANES:(c + 1) * HEAD_W]
            o_ref[:, c * HEAD_W:c * HEAD_W + LANES] = ((x1 * cos - x2 * sin) * scale).astype(o_ref.dtype)
            o_ref[:, c * HEAD_W + LANES:(c + 1) * HEAD_W] = ((x2 * cos + x1 * sin) * scale).astype(o_ref.dtype)


def _project(xb, wb, col0, ncols, kind, out_dtype, seq, tables=None):
    T, D = xb.shape
    tm = _tile(seq, 1024)
    tn = _tile(ncols, 1024)
    nb0 = col0 // tn
    in_specs = [pl.BlockSpec((tm, D), lambda i, j: (i, 0)),
                pl.BlockSpec((D, tn), lambda i, j: (0, j + nb0))]
    args = [xb, wb]
    if kind != "plain":
        npos = seq // tm
        in_specs += [pl.BlockSpec((tm, LANES), lambda i, j: (i % npos, 0))] * 2
        args += list(tables)
    return pl.pallas_call(
        functools.partial(_proj_kernel, kind=kind, n_q_blocks=GROUP_W // tn),
        grid=(T // tm, ncols // tn),
        in_specs=in_specs,
        out_specs=pl.BlockSpec((tm, tn), lambda i, j: (i, j)),
        out_shape=jax.ShapeDtypeStruct((T, ncols), out_dtype),
        compiler_params=_params(("parallel", "arbitrary")),
        name="in_proj_" + kind,
    )(*args)


def _rope_tables(seq, d):
    inv_freq = 1.0 / (ROPE_THETA ** (jnp.arange(0, d, 2, dtype=F32) / d))
    ang = jnp.arange(seq, dtype=F32)[:, None] * inv_freq[None, :]
    return jnp.cos(ang), jnp.sin(ang)


ATTN_ROW_CHUNK = 16


def _attn_kernel(q_ref, k_ref, v_ref, lam_ref, g_ref, o_ref, m_ref, l_ref, mp_ref, acc_ref, s_ref, p_ref,
                 *, tk, lam_init):
    seq = k_ref.shape[0]
    tq = q_ref.shape[0]
    rc = min(ATTN_ROW_CHUNK, tq)
    m_ref[...] = jnp.full(m_ref.shape, -jnp.inf, F32)
    l_ref[...] = jnp.zeros(l_ref.shape, F32)
    acc_ref[...] = jnp.zeros(acc_ref.shape, F32)

    nt = tk // LANES

    def lane_fold(x, op):
        y = x[:, :LANES]
        for t in range(1, nt):
            y = op(y, x[:, t * LANES:(t + 1) * LANES])
        return y

    def body(i, carry):
        off = pl.multiple_of(i * tk, tk)
        for c in range(2):
            s_ref[c] = lax.dot_general(q_ref[:, c * D_DIFF:(c + 1) * D_DIFF],
                                       k_ref[pl.ds(off, tk), c * D_DIFF:(c + 1) * D_DIFF],
                                       (((1,), (1,)), ((), ())), preferred_element_type=F32)
        for c in range(2):
            for r in range(tq // rc):
                rows = slice(r * rc, (r + 1) * rc)
                mp_ref[rows, :] = lane_fold(s_ref[c, rows, :], jnp.maximum)
            m_old = m_ref[c]
            m_new = jnp.maximum(m_old, jnp.max(mp_ref[...], axis=-1, keepdims=True))
            alpha = jnp.exp2(m_old - m_new)
            m_ref[c] = m_new
            for r in range(tq // rc):
                rows = slice(r * rc, (r + 1) * rc)
                mrow = m_ref[c, rows, :]
                p = jnp.exp2(s_ref[c, rows, :] - jnp.concatenate([mrow] * nt, axis=1))
                mp_ref[rows, :] = lane_fold(p, jnp.add)
                p_ref[c, rows, :] = p.astype(BF16)
            l_ref[c] = alpha * l_ref[c] + mp_ref[...]
            acc_ref[c] = (jnp.concatenate([alpha] * (HEAD_W // LANES), axis=1) * acc_ref[c]
                          + jnp.dot(p_ref[c], v_ref[pl.ds(off, tk), :], preferred_element_type=F32))
        return carry

    lax.fori_loop(0, seq // tk, body, 0)

    dl = lam_ref[...]
    lam = (jnp.exp(jnp.sum(dl[0:1] * dl[1:2], axis=-1, keepdims=True))
           - jnp.exp(jnp.sum(dl[2:3] * dl[3:4], axis=-1, keepdims=True)) + lam_init)
    l0 = jnp.sum(l_ref[0], axis=-1, keepdims=True)
    l1 = jnp.sum(l_ref[1], axis=-1, keepdims=True)
    o = acc_ref[0] / l0 - lam * (acc_ref[1] / l1)
    o = o * lax.rsqrt(jnp.mean(o * o, axis=-1, keepdims=True) + NORM_EPS)
    o_ref[...] = (o * g_ref[...] * (1.0 - lam_init)).astype(o_ref.dtype)


def _diff_attention(qk, v, diff_lambda, diff_norm_g, seq, lam_init):
    T = v.shape[0]
    nb = T // seq
    tq = _tile(seq, 1024)
    tk = _tile(seq, 2048)
    nq = seq // tq
    return pl.pallas_call(
        functools.partial(_attn_kernel, tk=tk, lam_init=lam_init),
        grid=(nb, H_DIFF, nq),
        in_specs=[pl.BlockSpec((tq, HEAD_W), lambda b, h, i: (b * nq + i, h)),
                  pl.BlockSpec((seq, HEAD_W), lambda b, h, i: (b, H_DIFF + h)),
                  pl.BlockSpec((seq, HEAD_W), lambda b, h, i: (b, h)),
                  pl.BlockSpec((4, D_DIFF), lambda b, h, i: (0, 0)),
                  pl.BlockSpec((1, HEAD_W), lambda b, h, i: (0, 0))],
        out_specs=pl.BlockSpec((tq, HEAD_W), lambda b, h, i: (b * nq + i, h)),
        out_shape=jax.ShapeDtypeStruct((T, GROUP_W), BF16),
        scratch_shapes=[pltpu.VMEM((2, tq, LANES), F32), pltpu.VMEM((2, tq, LANES), F32),
                        pltpu.VMEM((tq, LANES), F32),
                        pltpu.VMEM((2, tq, HEAD_W), F32), pltpu.VMEM((2, tq, tk), F32),
                        pltpu.VMEM((2, tq, tk), BF16)],
        compiler_params=_params(("parallel", "parallel", "arbitrary")),
        name="diff_attention",
    )(qk, qk, v, diff_lambda, diff_norm_g.reshape(1, HEAD_W))


RET_HEADS_PER_STEP = 2


def _ret_kernel(rate_ref, qf_ref, kf_ref, vf_ref, gf_ref, qb_ref, kb_ref, vb_ref, gb_ref, gn_ref,
                o_ref, st_ref, osum_ref, dec_ref, tab_ref, car_ref, *, cc, n):
    hps = RET_HEADS_PER_STEP
    h0 = (pl.program_id(0) % (H_RET // hps)) * hps
    c = pl.program_id(1)

    @pl.when(c == 0)
    def _():
        st_ref[...] = jnp.zeros(st_ref.shape, F32)
        ii = lax.broadcasted_iota(I32, (cc, cc), 0).astype(F32)
        jj = lax.broadcasted_iota(I32, (cc, cc), 1).astype(F32)
        dlt = ii - jj
        pos = lax.broadcasted_iota(I32, (cc, HEAD_W), 0).astype(F32)
        for hh in range(hps):
            def log_gamma(d, hh=hh):
                rate = jnp.full((1, 1), rate_ref[d, h0 + hh], F32)
                return jnp.log1p(-jnp.exp2(rate))

            lgf = log_gamma(0)
            lgb = log_gamma(1)
            dec_ref[hh, 0] = jnp.where(dlt >= 0, jnp.exp(jnp.where(dlt >= 0, dlt, 0.0) * lgf), 0.0)
            dec_ref[hh, 1] = jnp.where(dlt < 0, jnp.exp(jnp.where(dlt < 0, -dlt, 0.0) * lgb), 0.0)
            tab_ref[hh, 0] = jnp.exp((pos + 1.0) * lgf)
            tab_ref[hh, 1] = jnp.exp((cc - 1.0 - pos) * lgf)
            tab_ref[hh, 2] = jnp.exp((cc - pos) * lgb)
            tab_ref[hh, 3] = jnp.exp(pos * lgb)
            car_ref[hh, 0] = jnp.broadcast_to(jnp.exp(cc * lgf), car_ref.shape[2:])
            car_ref[hh, 1] = jnp.broadcast_to(jnp.exp(cc * lgb), car_ref.shape[2:])

    def direction(q_ref, k_ref, v_ref, hh, d):
        cols = slice(hh * HEAD_W, (hh + 1) * HEAD_W)
        q = q_ref[:, cols]
        k = k_ref[:, cols]
        v = v_ref[:, cols]
        s = lax.dot_general(q, k, (((1,), (1,)), ((), ())), preferred_element_type=F32) * dec_ref[hh, d]
        state = st_ref[hh, d]
        o = (jnp.dot(s.astype(BF16), v, preferred_element_type=F32)
             + jnp.dot(q, state.astype(BF16), preferred_element_type=F32) * tab_ref[hh, 2 * d])
        kz = (k.astype(F32) * tab_ref[hh, 2 * d + 1]).astype(BF16)
        st_ref[hh, d] = car_ref[hh, d, 0:1, 0:1] * state + lax.dot_general(
            kz, v, (((0,), (0,)), ((), ())), preferred_element_type=F32)
        return o

    rows_f = pl.ds(pl.multiple_of(c * cc, cc), cc)
    rows_b = pl.ds(pl.multiple_of((n - 1 - c) * cc, cc), cc)

    def finish(o, gate_ref, cols):
        mu = jnp.mean(o, axis=-1, keepdims=True)
        var = jnp.mean(jnp.square(o - mu), axis=-1, keepdims=True)
        y = (o - mu) * lax.rsqrt(var + NORM_EPS) * gn_ref[:, cols]
        gate = gate_ref[:, cols]
        return (y * (gate * _sigmoid(gate))).astype(o_ref.dtype)

    outs = [(direction(qf_ref, kf_ref, vf_ref, hh, 0), direction(qb_ref, kb_ref, vb_ref, hh, 1))
            for hh in range(hps)]

    @pl.when(c < n // 2)
    def _():
        for hh, (o_f, o_b) in enumerate(outs):
            cols = slice(hh * HEAD_W, (hh + 1) * HEAD_W)
            osum_ref[rows_f, cols] = o_f
            osum_ref[rows_b, cols] = o_b

    @pl.when(c >= n // 2)
    def _():
        for hh, (o_f, o_b) in enumerate(outs):
            cols = slice(hh * HEAD_W, (hh + 1) * HEAD_W)
            o_ref[rows_f, cols] = finish(osum_ref[rows_f, cols] + o_f, gf_ref, cols)
            o_ref[rows_b, cols] = finish(osum_ref[rows_b, cols] + o_b, gb_ref, cols)


def _retention(qk, v, gate, ret_log2_decay, ret_norm_g, seq):
    T = v.shape[0]
    nb = T // seq
    cc = _tile(seq // 2, 256)
    n = seq // cc
    assert n % 2 == 0

    hps = RET_HEADS_PER_STEP
    hg = H_RET // hps
    bw = hps * HEAD_W

    def fwd(col):
        return pl.BlockSpec((cc, bw), lambda bh, c: ((bh // hg) * n + c, (bh % hg) + col))

    def bwd(col):
        return pl.BlockSpec((cc, bw), lambda bh, c: ((bh // hg) * n + n - 1 - c, (bh % hg) + col))

    return pl.pallas_call(
        functools.partial(_ret_kernel, cc=cc, n=n),
        grid=(nb * hg, n),
        in_specs=[pl.BlockSpec(memory_space=pltpu.SMEM),
                  fwd(0), fwd(hg), fwd(0), fwd(0), bwd(0), bwd(hg), bwd(0), bwd(0),
                  pl.BlockSpec((1, bw), lambda bh, c: (0, bh % hg))],
        out_specs=pl.BlockSpec((seq, bw), lambda bh, c: (bh // hg, bh % hg)),
        out_shape=jax.ShapeDtypeStruct((T, GROUP_W), BF16),
        scratch_shapes=[pltpu.VMEM((hps, 2, DK_RET, DV_RET), F32),
                        pltpu.VMEM((seq, bw), F32), pltpu.VMEM((hps, 2, cc, cc), F32),
                        pltpu.VMEM((hps, 4, cc, HEAD_W), F32), pltpu.VMEM((hps, 2, 8, LANES), F32)],
        compiler_params=_params(("parallel", "arbitrary")),
        name="retention",
    )(ret_log2_decay, qk, qk, v, gate, qk, qk, v, gate, ret_norm_g.reshape(1, GROUP_W))


def _outproj_kernel(a1_ref, a2_ref, w1_ref, w2_ref, o_ref):
    o_ref[...] = (jnp.dot(a1_ref[...], w1_ref[...], preferred_element_type=F32)
                  + jnp.dot(a2_ref[...], w2_ref[...], preferred_element_type=F32))


def _out_projection(od, orr, wob):
    T = od.shape[0]
    D = wob.shape[1]
    tm = _tile(T, 1024)
    tn = _tile(D, 1024)
    return pl.pallas_call(
        _outproj_kernel,
        grid=(T // tm, D // tn),
        in_specs=[pl.BlockSpec((tm, GROUP_W), lambda i, j: (i, 0)),
                  pl.BlockSpec((tm, GROUP_W), lambda i, j: (i, 0)),
                  pl.BlockSpec((GROUP_W, tn), lambda i, j: (0, j)),
                  pl.BlockSpec((GROUP_W, tn), lambda i, j: (1, j))],
        out_specs=pl.BlockSpec((tm, tn), lambda i, j: (i, j)),
        out_shape=jax.ShapeDtypeStruct((T, D), F32),
        compiler_params=_params(("parallel", "arbitrary")),
        name="out_proj",
    )(od, orr, wob, wob)


def _layer_norm_rows(z, g, b):
    mu = jnp.mean(z, axis=-1, keepdims=True)
    var = jnp.mean(jnp.square(z - mu), axis=-1, keepdims=True)
    return (z - mu) * lax.rsqrt(var + NORM_EPS) * g + b


def _ln1_kernel(*refs, tile_starts):
    ng = len(tile_starts) - 1
    x_refs = refs[:ng]
    mix_ref, g_ref, b_ref, wr_ref, x1_ref, lg_ref = refs[ng:]
    i = pl.program_id(0)
    for gi, x_ref in enumerate(x_refs):
        @pl.when((i >= tile_starts[gi]) & (i < tile_starts[gi + 1]))
        def _(x_ref=x_ref):
            y = _layer_norm_rows(DEEPNORM_ALPHA * x_ref[...] + mix_ref[...], g_ref[...], b_ref[...])
            x1_ref[...] = y
            lg_ref[...] = lax.dot_general(wr_ref[...], y, (((1,), (1,)), ((), ())),
                                          precision=lax.Precision.HIGHEST, preferred_element_type=F32)


def _ln1_router(x_groups2d, mix, g, b, w_router_t):
    T, D = mix.shape
    tr = _tile(math.gcd(*[x.shape[0] for x in x_groups2d]), 256)
    tile_starts = [0]
    for x in x_groups2d:
        tile_starts.append(tile_starts[-1] + x.shape[0] // tr)

    def x_spec(gi):
        lo, hi = tile_starts[gi], tile_starts[gi + 1]
        return pl.BlockSpec((tr, D), lambda i: (jnp.clip(i - lo, 0, hi - lo - 1), 0))

    return pl.pallas_call(
        functools.partial(_ln1_kernel, tile_starts=tuple(tile_starts)),
        grid=(T // tr,),
        in_specs=[x_spec(gi) for gi in range(len(x_groups2d))] + [
                  pl.BlockSpec((tr, D), lambda i: (i, 0)),
                  pl.BlockSpec((1, D), lambda i: (0, 0)),
                  pl.BlockSpec((1, D), lambda i: (0, 0)),
                  pl.BlockSpec((N_EXPERTS, D), lambda i: (0, 0))],
        out_specs=[pl.BlockSpec((tr, D), lambda i: (i, 0)),
                   pl.BlockSpec((N_EXPERTS, tr), lambda i: (0, i))],
        out_shape=[jax.ShapeDtypeStruct((T, D), F32), jax.ShapeDtypeStruct((N_EXPERTS, T), F32)],
        compiler_params=_params(("parallel",)),
        name="ln1_router",
    )(*x_groups2d, mix, g.reshape(1, D), b.reshape(1, D), w_router_t)


def _route_kernel(lg_ref, idx_ref, pk_ref, pos_ref, gk_ref, mk_ref, kmax_ref, *, cap, tok0):
    E, R, _ = lg_ref.shape
    logits = lg_ref[...]
    ex = jnp.exp(logits - jnp.max(logits, axis=0, keepdims=True))
    aff = ex / jnp.sum(ex, axis=0, keepdims=True)
    bits = pltpu.bitcast(aff, I32)

    def count(pred):
        c = jnp.sum(jnp.where(pred, 1.0, 0.0), axis=2, keepdims=True)
        return jnp.sum(c, axis=1, keepdims=True)

    capf = jnp.float32(cap)

    def thr_step(i, thr):
        cand = thr | lax.shift_left(jnp.int32(1), 30 - i)
        return jnp.where(count(bits >= cand) >= capf, cand, thr)

    thr = lax.fori_loop(0, 31, thr_step, jnp.zeros((E, 1, 1), I32))
    gt = bits > thr
    eq = bits == thr
    need = capf - count(gt)
    tid = (lax.broadcasted_iota(I32, (E, R, LANES), 1) * LANES
           + lax.broadcasted_iota(I32, (E, R, LANES), 2))
    nbits = int(math.log2(R * LANES))

    def tie_step(i, v):
        cand = v | lax.shift_left(jnp.int32(1), nbits - 1 - i)
        return jnp.where(count(eq & (tid < cand)) < need, cand, v)

    vtie = lax.fori_loop(0, nbits, tie_step, jnp.zeros((E, 1, 1), I32))
    mask = jnp.where(gt | (eq & (tid <= vtie)), 1.0, 0.0)

    ranks = []
    cnt = jnp.zeros((R, LANES), F32)
    for e in range(E):
        ranks.append(cnt)
        cnt = cnt + mask[e]
    for k in range(E):
        gk = jnp.zeros((R, LANES), F32)
        for e in range(k, E):
            gk = gk + jnp.where((ranks[e] == k) & (mask[e] > 0), aff[e], 0.0)
        gk_ref[k] = gk
        mk_ref[k] = jnp.where(cnt > k, 1.0, 0.0)
    kmax_ref[...] = jnp.zeros(kmax_ref.shape, I32)
    for t in range(LANES // COMBINE_TOKENS):
        tile_cnt = cnt[:, t * COMBINE_TOKENS:(t + 1) * COMBINE_TOKENS]
        kmax_ref[:, t:t + 1] = jnp.max(tile_cnt, axis=1, keepdims=True).astype(I32)
    lane_col = lax.broadcasted_iota(I32, (LANES, 1), 0).astype(F32)

    mb = mask.astype(BF16)
    li = lax.broadcasted_iota(I32, (LANES, LANES), 0)
    lj = lax.broadcasted_iota(I32, (LANES, LANES), 1)
    upper = jnp.where(li <= lj, 1.0, 0.0).astype(BF16)
    lower = jnp.where(li >= lj, 1.0, 0.0).astype(BF16)
    ri = lax.broadcasted_iota(I32, (R, R), 0)
    rj = lax.broadcasted_iota(I32, (R, R), 1)
    strict = jnp.where(rj < ri, 1.0, 0.0).astype(BF16)
    slot = lax.broadcasted_iota(I32, (1, cap), 1).astype(F32)
    rcol = lax.broadcasted_iota(I32, (R, 1), 0).astype(F32)
    for e in range(E):
        me = mb[e]
        winc = jnp.dot(me, upper, preferred_element_type=F32)
        rows = winc[:, LANES - 1:LANES]
        offs = jnp.sum(jnp.dot(strict, me, preferred_element_type=F32), axis=-1, keepdims=True)
        pos_ref[e] = (offs + winc - mask[e]).astype(I32)
        ends = offs + rows
        rs = jnp.sum(jnp.where(ends <= slot, 1.0, 0.0), axis=0, keepdims=True)
        onehot = rcol == rs
        offs_s = jnp.sum(jnp.where(onehot, offs, 0.0), axis=0, keepdims=True)
        winc_t = lax.dot_general(lower, me, (((1,), (1,)), ((), ())), preferred_element_type=F32)
        wg = jnp.dot(winc_t.astype(BF16), jnp.where(onehot, 1.0, 0.0).astype(BF16),
                     preferred_element_type=F32)
        ls = jnp.sum(jnp.where(wg <= slot - offs_s, 1.0, 0.0), axis=0, keepdims=True)
        tok = (rs * LANES + ls).astype(I32)
        idx_ref[e:e + 1, :] = tok + tok0
        rg = lax.dot_general(ranks[e].astype(BF16), jnp.where(onehot, 1.0, 0.0).astype(BF16),
                             (((0,), (0,)), ((), ())), preferred_element_type=F32)
        rk = jnp.sum(jnp.where(lane_col == ls, rg, 0.0), axis=0, keepdims=True).astype(I32)
        pk_ref[e:e + 1, :] = tok * E + rk


def _route(logits_t, cap, tok0):
    E, Tg = logits_t.shape
    R = Tg // LANES
    per_tok = jax.ShapeDtypeStruct((E, R, LANES), F32)
    idx, pk, pos, gk, mk, kmax = pl.pallas_call(
        functools.partial(_route_kernel, cap=cap, tok0=tok0),
        out_shape=[jax.ShapeDtypeStruct((E, cap), I32), jax.ShapeDtypeStruct((E, cap), I32),
                   jax.ShapeDtypeStruct((E, R, LANES), I32), per_tok, per_tok,
                   jax.ShapeDtypeStruct((R, LANES), I32)],
        compiler_params=pltpu.CompilerParams(vmem_limit_bytes=V7X_VMEM_LIMIT_BYTES),
        name="route",
    )(logits_t.reshape(E, R, LANES))
    kmax = kmax[:, :LANES // COMBINE_TOKENS].reshape(-1)
    return idx, pk, pos.reshape(E, Tg), gk.reshape(E, Tg), mk.reshape(E, Tg), kmax


def _gather_kernel(idx_ref, idx_next_ref, x_hbm, o_ref, buf, sem):
    i = pl.program_id(0)
    tg = buf.shape[1]

    def row_copy(src_row, slot, dst_row):
        return pltpu.make_async_copy(x_hbm.at[pl.ds(src_row, 1)], buf.at[slot, pl.ds(dst_row, 1)], sem.at[slot])

    def issue(ref, slot):
        def body(r, carry):
            row_copy(ref[0, 0, r], slot, r).start()
            return carry
        lax.fori_loop(0, tg, body, 0, unroll=8)

    @pl.when(i == 0)
    def _():
        issue(idx_ref, 0)

    @pl.when(i + 1 < pl.num_programs(0))
    def _():
        issue(idx_next_ref, (i + 1) % 2)

    slot = i % 2
    for _ in range(tg):
        row_copy(0, slot, 0).wait()
    o_ref[...] = buf[slot].astype(o_ref.dtype)


def _gather_tokens(x1, idx_all):
    T, D = x1.shape
    n_rows = idx_all.size
    tg = _tile(n_rows, 256)
    n_tiles = n_rows // tg
    idx_tiles = idx_all.reshape(n_tiles, 1, tg)
    return pl.pallas_call(
        _gather_kernel,
        grid=(n_tiles,),
        in_specs=[pl.BlockSpec((1, 1, tg), lambda i: (i, 0, 0), memory_space=pltpu.SMEM),
                  pl.BlockSpec((1, 1, tg), lambda i: (jnp.minimum(i + 1, n_tiles - 1), 0, 0),
                               memory_space=pltpu.SMEM),
                  pl.BlockSpec(memory_space=pl.ANY)],
        out_specs=pl.BlockSpec((tg, D), lambda i: (i, 0)),
        out_shape=jax.ShapeDtypeStruct((n_rows, D), BF16),
        scratch_shapes=[pltpu.VMEM((2, tg, D), F32), pltpu.SemaphoreType.DMA((2,))],
        compiler_params=_params(("arbitrary",)),
        name="gather_tokens",
    )(idx_tiles, idx_tiles, x1)


def _ffn_a_kernel(x_ref, wg_ref, wu_ref, h_ref):
    x = x_ref[...]
    g = jnp.dot(x, wg_ref[...].astype(BF16), preferred_element_type=F32)
    u = jnp.dot(x, wu_ref[...].astype(BF16), preferred_element_type=F32)
    h_ref[...] = (g * _sigmoid(g) * u).astype(h_ref.dtype)


def _ffn_hidden(xe, w_gate, w_up):
    E, Ctot, D = xe.shape
    F = w_gate.shape[2]
    tm = _tile(Ctot, 1536)
    tf = _tile(F, 256)
    return pl.pallas_call(
        _ffn_a_kernel,
        grid=(E, Ctot // tm, F // tf),
        in_specs=[pl.BlockSpec((None, tm, D), lambda e, m, f: (e, m, 0)),
                  pl.BlockSpec((None, D, tf), lambda e, m, f: (e, 0, f)),
                  pl.BlockSpec((None, D, tf), lambda e, m, f: (e, 0, f))],
        out_specs=pl.BlockSpec((None, tm, tf), lambda e, m, f: (e, m, f)),
        out_shape=jax.ShapeDtypeStruct((E, Ctot, F), BF16),
        compiler_params=_params(("parallel", "parallel", "arbitrary")),
        name="ffn_hidden",
    )(xe, w_gate, w_up)


def _ffn_b_kernel(h_ref, wd_ref, y_ref):
    @pl.when(pl.program_id(3) == 0)
    def _():
        y_ref[...] = jnp.zeros(y_ref.shape, F32)

    y_ref[...] += jnp.dot(h_ref[...], wd_ref[...].astype(BF16), preferred_element_type=F32)


def _ffn_out(h, w_down):
    E, Ctot, F = h.shape
    D = w_down.shape[2]
    tm = _tile(Ctot, 1536)
    tn = _tile(D, 1024)
    tk = _tile(F, 2048)
    return pl.pallas_call(
        _ffn_b_kernel,
        grid=(E, Ctot // tm, D // tn, F // tk),
        in_specs=[pl.BlockSpec((None, tm, tk), lambda e, m, n, k: (e, m, k)),
                  pl.BlockSpec((None, tk, tn), lambda e, m, n, k: (e, k, n))],
        out_specs=pl.BlockSpec((None, tm, tn), lambda e, m, n, k: (e, m, n)),
        out_shape=jax.ShapeDtypeStruct((E, Ctot, D), F32),
        compiler_params=_params(("parallel", "parallel", "parallel", "arbitrary")),
        name="ffn_out",
    )(h, w_down)


def _combine_kernel(starts_ref, pk_ref, kmax_ref, ye_hbm, gk_ref, mk_ref, x1_ref, g_ref, b_ref, o_ref, buf, sem,
                    *, row0, ctot):
    E = buf.shape[1]
    tt = x1_ref.shape[0]
    j = pl.program_id(0)

    def row_copy(src_row, slot, k, dst_row):
        return pltpu.make_async_copy(ye_hbm.at[pl.ds(src_row, 1)], buf.at[slot, k, pl.ds(dst_row, 1)],
                                     sem.at[slot])

    def issue(tile, slot):
        for e in range(E):
            s0 = starts_ref[e, tile]
            n = starts_ref[e, tile + 1] - s0

            def one(s, e=e):
                word = pk_ref[e, s]
                tok = lax.shift_right_logical(word, int(math.log2(E)))
                row_copy(row0 + e * ctot + s, slot, word & (E - 1), tok - tile * tt).start()

            def body4(i, carry, s0=s0, one=one):
                for u in range(4):
                    one(s0 + 4 * i + u)
                return carry

            def body1(i, carry, s0=s0, n=n, one=one):
                one(s0 + (n & ~3) + i)
                return carry

            lax.fori_loop(0, lax.shift_right_logical(n, 2), body4, 0)
            lax.fori_loop(0, n & 3, body1, 0)

    @pl.when(j == 0)
    def _():
        buf[...] = jnp.zeros(buf.shape, F32)
        issue(0, 0)

    @pl.when(j + 1 < pl.num_programs(0))
    def _():
        issue(j + 1, (j + 1) % 2)

    slot = j % 2
    n_rows = jnp.int32(0)
    for e in range(E):
        n_rows = n_rows + (starts_ref[e, j + 1] - starts_ref[e, j])

    def drain8(r, carry):
        for _ in range(8):
            row_copy(0, slot, 0, 0).wait()
        return carry

    def drain1(r, carry):
        row_copy(0, slot, 0, 0).wait()
        return carry

    lax.fori_loop(0, lax.shift_right_logical(n_rows, 3), drain8, 0)
    lax.fori_loop(0, n_rows & 7, drain1, 0)

    gk = gk_ref[...]
    mk = mk_ref[...]
    n_k = kmax_ref[j]

    def picked(k):
        return jnp.where(mk[:, k:k + 1] > 0, buf[slot, k] * gk[:, k:k + 1], 0.0)

    o_ref[...] = DEEPNORM_ALPHA * x1_ref[...] + (picked(0) + picked(1))
    for k in range(2, E, 2):
        @pl.when(k < n_k)
        def _(k=k):
            o_ref[...] += picked(k) + picked(k + 1)
    o_ref[...] = _layer_norm_rows(o_ref[...], g_ref[...], b_ref[...])


def _combine(ye_rows, starts, pk, kmax, gk_t, mk_t, x1, g, b, tok0, row0, ctot):
    Tg = gk_t.shape[0]
    D = x1.shape[1]
    tt = COMBINE_TOKENS
    blk0 = tok0 // tt
    grid_spec = pltpu.PrefetchScalarGridSpec(
        num_scalar_prefetch=3,
        grid=(Tg // tt,),
        in_specs=[pl.BlockSpec(memory_space=pl.ANY),
                  pl.BlockSpec((tt, N_EXPERTS), lambda j, *_: (j, 0)),
                  pl.BlockSpec((tt, N_EXPERTS), lambda j, *_: (j, 0)),
                  pl.BlockSpec((tt, D), lambda j, *_: (j + blk0, 0)),
                  pl.BlockSpec((1, D), lambda j, *_: (0, 0)),
                  pl.BlockSpec((1, D), lambda j, *_: (0, 0))],
        out_specs=pl.BlockSpec((tt, D), lambda j, *_: (j, 0)),
        scratch_shapes=[pltpu.VMEM((2, N_EXPERTS, tt, D), F32), pltpu.SemaphoreType.DMA((2,))],
    )
    return pl.pallas_call(
        functools.partial(_combine_kernel, row0=row0, ctot=ctot),
        grid_spec=grid_spec,
        out_shape=jax.ShapeDtypeStruct((Tg, D), F32),
        compiler_params=_params(("arbitrary",)),
        name="combine_ln2",
    )(starts, pk, kmax, ye_rows, gk_t, mk_t, x1, g.reshape(1, D), b.reshape(1, D))


def _encoder_layer(x_groups, layer_idx, w_in, w_out, diff_lambda, diff_norm_g, ret_log2_decay, ret_norm_g,
                   ln_g, ln_b, w_router, w_gate, w_up, w_down):
    seq = x_groups[0].shape[1]
    D = x_groups[0].shape[2]
    assert all(x.shape[1] == seq for x in x_groups)
    lam_init = 0.8 - 0.6 * math.exp(-0.3 * layer_idx)
    x2d = [g.reshape(-1, D) for g in x_groups]
    xb = jnp.concatenate([g.astype(BF16) for g in x2d], axis=0)
    wb = w_in.astype(BF16)
    wob = w_out.astype(BF16)

    c128, s128 = _rope_tables(seq, D_DIFF)
    c256, s256 = _rope_tables(seq, DK_RET)
    tab128 = (jnp.concatenate([c128, c128], axis=1), jnp.concatenate([-s128, s128], axis=1))
    tab256 = (c256, s256)

    qkd = _project(xb, wb, 0 * GROUP_W, 2 * GROUP_W, "rot128", BF16, seq, tab128)
    vd = _project(xb, wb, 2 * GROUP_W, GROUP_W, "plain", BF16, seq)
    qkr = _project(xb, wb, 3 * GROUP_W, 2 * GROUP_W, "rot256", BF16, seq, tab256)
    vr = _project(xb, wb, 5 * GROUP_W, GROUP_W, "plain", BF16, seq)
    gr = _project(xb, wb, 6 * GROUP_W, GROUP_W, "plain", F32, seq)

    od = _diff_attention(qkd, vd, diff_lambda, diff_norm_g, seq, lam_init)
    orr = _retention(qkr, vr, gr, ret_log2_decay, ret_norm_g, seq)
    mix = _out_projection(od, orr, wob)
    x1, logits_t = _ln1_router(x2d, mix, ln_g[0], ln_b[0], w_router.T)

    group_tokens = [g.shape[0] * seq for g in x_groups]
    caps = [min(t, max(1, EC_CAPACITY * t // N_EXPERTS)) for t in group_tokens]
    ctot = sum(caps)
    routed = []
    tok0 = 0
    for tg, cap in zip(group_tokens, caps):
        routed.append(_route(logits_t[:, tok0:tok0 + tg], cap, tok0))
        tok0 += tg
    idx_all = jnp.concatenate([r[0] for r in routed], axis=1)
    xe = _gather_tokens(x1, idx_all).reshape(N_EXPERTS, ctot, D)
    h = _ffn_hidden(xe, w_gate, w_up)
    ye = _ffn_out(h, w_down).reshape(N_EXPERTS * ctot, D)

    outs = []
    tok0 = 0
    row0 = 0
    for g, tg, cap, (_, pk, pos, gk, mk, kmax) in zip(x_groups, group_tokens, caps, routed):
        starts = jnp.concatenate([pos[:, ::COMBINE_TOKENS], jnp.full((N_EXPERTS, 1), cap, I32)], axis=1)
        y = _combine(ye, starts, pk, kmax, gk.T, mk.T, x1, ln_g[1], ln_b[1], tok0, row0, ctot)
        outs.append(y.reshape(g.shape))
        tok0 += tg
        row0 += cap
    return outs


def kernel(x_prompt, x_sample, w_in, w_out, diff_lambda, diff_norm_g, ret_log2_decay, ret_norm_g, ln_g, ln_b,
           w_router, w_gate, w_up, w_down):
    xs = [x_prompt, x_sample]
    for l in range(w_in.shape[0]):
        xs = _encoder_layer(xs, l, w_in[l], w_out[l], diff_lambda[l], diff_norm_g[l], ret_log2_decay[l],
                            ret_norm_g[l], ln_g[l], ln_b[l], w_router[l], w_gate[l], w_up[l], w_down[l])
    return (xs[0], xs[1])
```
